```python
import math
import jax
import jax.numpy as jnp
from jax import lax
import numpy as np

D_MODEL = 2048
BATCH = 8
SEQ = 4096
DEPTH = 2

GRID_W = 64
CTX_LEN = 256
N_MIXERS = 2
N_CONV_LAYERS = (DEPTH + 1) // 2
N_ATTN_LAYERS = DEPTH // 2
N_SUB = 3
FFN_DIM = 5632
CONV_WIDTH = 31
DA_HEADS = 16
DA_HEAD_DIM = D_MODEL // DA_HEADS // 2
AXIS_DIM = DA_HEAD_DIM // 2
ROPE_THETA = 10000.0
Q_BLOCK = 128
NORM_EPS = 1e-6
SUBLN_EPS = 1e-5
LN_EPS = 1e-5

kernel_name = 'hybrid_conformer_diffattn_block'


def _rms_norm(x, g, eps=NORM_EPS):
    xf = x.astype(jnp.float32)
    y = xf * lax.rsqrt(jnp.mean(xf * xf, axis=-1, keepdims=True) + eps)
    return (y * g.astype(jnp.float32)).astype(x.dtype)


def _layer_norm(x, g, b, eps=LN_EPS):
    xf = x.astype(jnp.float32)
    mu = jnp.mean(xf, axis=-1, keepdims=True)
    var = jnp.mean(jnp.square(xf - mu), axis=-1, keepdims=True)
    y = (xf - mu) * lax.rsqrt(var + eps)
    return (y * g.astype(jnp.float32) + b.astype(jnp.float32)).astype(x.dtype)


def _pre(x, g, mod, j):
    return _rms_norm(x, g) * (1 + mod[3 * j + 1]) + mod[3 * j]


def _post(x, y, g, mod, j, weight):
    return x + weight * mod[3 * j + 2] * _rms_norm(y, g)


def _swiglu(h, w_in, w_out):
    a, u = jnp.split(h @ w_in, 2, axis=-1)
    return (jax.nn.silu(a) * u) @ w_out


def _half_ffn(x, mod, j, g_pre, g_post, w_in, w_out):
    return _post(x, _swiglu(_pre(x, g_pre, mod, j), w_in, w_out), g_post, mod, j, 0.5)


def _conformer_conv(h, w_pw1, b_pw1, w_dw, b_dw, ln_g, ln_b, w_pw2, b_pw2):
    a, g = jnp.split(h @ w_pw1 + b_pw1, 2, axis=-1)
    u = a * jax.nn.sigmoid(g)
    pad = CONV_WIDTH // 2
    u = lax.conv_general_dilated(u, w_dw[:, None, :].astype(u.dtype), window_strides=(1,),
                                 padding=[(pad, pad)], dimension_numbers=('NWC', 'WIO', 'NWC'),
                                 feature_group_count=u.shape[-1]) + b_dw
    u = jax.nn.silu(_layer_norm(u, ln_g, ln_b))
    return u @ w_pw2 + b_pw2


def _axial_rope_tables(n_tokens, dtype):
    t = jnp.arange(n_tokens, dtype=jnp.int32)
    inv_freq = ROPE_THETA ** (-jnp.arange(0, AXIS_DIM, 2, dtype=jnp.float32) / AXIS_DIM)
    tabs = []
    for pos in (t // GRID_W, t % GRID_W):
        ang = (pos.astype(jnp.float32)[:, None] * inv_freq)[None, :, None, None, :]
        tabs += [jnp.cos(ang).astype(dtype), jnp.sin(ang).astype(dtype)]
    return tabs


def _rotate_half(x, cos, sin):
    x1, x2 = jnp.split(x, 2, axis=-1)
    return jnp.concatenate([x1 * cos - x2 * sin, x1 * sin + x2 * cos], axis=-1)


def _axial_rope(x, tabs):
    cos_r, sin_r, cos_c, sin_c = tabs
    return jnp.concatenate([_rotate_half(x[..., :AXIS_DIM], cos_r, sin_r),
                            _rotate_half(x[..., AXIS_DIM:], cos_c, sin_c)], axis=-1)


def _diff_weights(s, lam):
    p = jax.nn.softmax(s.astype(jnp.float32), axis=-1)
    return p[:, :, 0] - lam * p[:, :, 1]


def _diff_attn_latent(q, k_all, v_all, lam):
    b, n, h, _, dh = q.shape
    qb = jnp.moveaxis(q.reshape(b, n // Q_BLOCK, Q_BLOCK, h, 2, dh), 1, 0)

    def block(q_blk):
        s = jnp.einsum('bqhcd,bkhcd->bhcqk', q_blk, k_all)
        a = _diff_weights(s, lam).astype(v_all.dtype)
        return jnp.einsum('bhqk,bkhe->bqhe', a, v_all)

    o = lax.map(block, qb)
    return jnp.moveaxis(o, 0, 1).reshape(b, n, h, v_all.shape[-1])


def _diff_head_out(o, subln_g, lam_init, w_o):
    b, n = o.shape[:2]
    o = _rms_norm(o, subln_g, SUBLN_EPS) * (1.0 - lam_init)
    return o.reshape(b, n, -1) @ w_o


def _diff_attention(hl, hc, w_qkv, lq1, lk1, lq2, lk2, subln_g, w_o, lam_init, rope, with_ctx):
    b, n, d = hl.shape
    m = hc.shape[1]
    h, dh = DA_HEADS, DA_HEAD_DIM
    scale = dh ** -0.5
    ql, kl, vl = jnp.split(hl @ w_qkv, 3, axis=-1)
    ql = _axial_rope(ql.reshape(b, n, h, 2, dh), rope) * scale
    kl = _axial_rope(kl.reshape(b, n, h, 2, dh), rope)
    vl = vl.reshape(b, n, h, 2 * dh)
    if with_ctx:
        qc, kc, vc = jnp.split(hc @ w_qkv, 3, axis=-1)
    else:
        kc, vc = jnp.split(hc @ w_qkv[:, d:], 2, axis=-1)
    kc = kc.reshape(b, m, h, 2, dh)
    vc = vc.reshape(b, m, h, 2 * dh)
    lam = (jnp.exp(jnp.sum(lq1.astype(jnp.float32) * lk1.astype(jnp.float32)))
           - jnp.exp(jnp.sum(lq2.astype(jnp.float32) * lk2.astype(jnp.float32))) + lam_init)
    k_all = jnp.concatenate([kl, kc], axis=1)
    v_all = jnp.concatenate([vl, vc], axis=1)
    yl = _diff_head_out(_diff_attn_latent(ql, k_all, v_all, lam), subln_g, lam_init, w_o)
    if not with_ctx:
        return yl, None
    qc = qc.reshape(b, m, h, 2, dh) * scale
    s = jnp.einsum('bqhcd,bkhcd->bhcqk', qc, kc)
    oc = jnp.einsum('bhqk,bkhe->bqhe', _diff_weights(s, lam).astype(vc.dtype), vc)
    return yl, _diff_head_out(oc, subln_g, lam_init, w_o)


def setup_inputs(seed: int = 0) -> dict:
    key = jax.random.key(seed)
    ks = iter(jax.random.split(key, 32))

    def nrm(shape, scale):
        return scale * jax.random.normal(next(ks), shape, dtype=jnp.float32)

    d, f, h2 = D_MODEL, FFN_DIM, 2 * DA_HEAD_DIM
    nc, na = N_CONV_LAYERS, N_ATTN_LAYERS
    return {
        'x': nrm((BATCH, SEQ, d), 1.0),
        'c': nrm((BATCH, d), 1.0),
        'ctx': nrm((BATCH, CTX_LEN, d), 1.0),
        'c_ctx': nrm((d,), 1.0),
        'ada_w': nrm((DEPTH, d, 3 * N_SUB * d), 0.5 * d ** -0.5),
        'ada_b': nrm((DEPTH, 3 * N_SUB * d), 0.01),
        'norm_pre': 1.0 + nrm((DEPTH, N_SUB, d), 0.05),
        'norm_post': 1.0 + nrm((DEPTH, N_SUB, d), 0.05),
        'ffn1_w_in': nrm((DEPTH, d, 2 * f), d ** -0.5),
        'ffn1_w_out': nrm((DEPTH, f, d), f ** -0.5),
        'ffn2_w_in': nrm((DEPTH, d, 2 * f), d ** -0.5),
        'ffn2_w_out': nrm((DEPTH, f, d), f ** -0.5),
        'conv_w_pw1': nrm((nc, d, 2 * d), d ** -0.5),
        'conv_b_pw1': nrm((nc, 2 * d), 0.01),
        'conv_w_dw': nrm((nc, CONV_WIDTH, d), CONV_WIDTH ** -0.5),
        'conv_b_dw': nrm((nc, d), 0.01),
        'conv_ln_g': 1.0 + nrm((nc, d), 0.05),
        'conv_ln_b': nrm((nc, d), 0.01),
        'conv_w_pw2': nrm((nc, d, d), d ** -0.5),
        'conv_b_pw2': nrm((nc, d), 0.01),
        'attn_w_qkv': nrm((na, d, 3 * d), d ** -0.5),
        'attn_lambda_q1': nrm((na, DA_HEAD_DIM), 0.1),
        'attn_lambda_k1': nrm((na, DA_HEAD_DIM), 0.1),
        'attn_lambda_q2': nrm((na, DA_HEAD_DIM), 0.1),
        'attn_lambda_k2': nrm((na, DA_HEAD_DIM), 0.1),
        'attn_subln_g': 1.0 + nrm((na, h2), 0.05),
        'attn_w_o': nrm((na, d, d), d ** -0.5),
    }


def reference(x, c, ctx, c_ctx, ada_w, ada_b, norm_pre, norm_post, ffn1_w_in, ffn1_w_out, ffn2_w_in, ffn2_w_out,
              conv_w_pw1, conv_b_pw1, conv_w_dw, conv_b_dw, conv_ln_g, conv_ln_b, conv_w_pw2, conv_b_pw2,
              attn_w_qkv, attn_lambda_q1, attn_lambda_k1, attn_lambda_q2, attn_lambda_k2, attn_subln_g, attn_w_o):
    b, n, d = x.shape
    rope = _axial_rope_tables(n, x.dtype)
    xc = ctx
    for i in range(DEPTH):
        last = i == DEPTH - 1
        is_conv = i % N_MIXERS == 0
        li = i // N_MIXERS
        mod_l = (jax.nn.silu(c) @ ada_w[i] + ada_b[i]).reshape(b, 3 * N_SUB, d).transpose(1, 0, 2)[:, :, None, :]
        mod_c = (jax.nn.silu(c_ctx) @ ada_w[i] + ada_b[i]).reshape(3 * N_SUB, 1, 1, d)
        g_pre, g_post = norm_pre[i], norm_post[i]
        ctx_live = not (last and is_conv)

        x = _half_ffn(x, mod_l, 0, g_pre[0], g_post[0], ffn1_w_in[i], ffn1_w_out[i])
        if ctx_live:
            xc = _half_ffn(xc, mod_c, 0, g_pre[0], g_post[0], ffn1_w_in[i], ffn1_w_out[i])

        hl = _pre(x, g_pre[1], mod_l, 1)
        if is_conv:
            conv_p = (conv_w_pw1[li], conv_b_pw1[li], conv_w_dw[li], conv_b_dw[li],
                      conv_ln_g[li], conv_ln_b[li], conv_w_pw2[li], conv_b_pw2[li])
            x = _post(x, _conformer_conv(hl, *conv_p), g_post[1], mod_l, 1, 1.0)
            if not last:
                yc = _conformer_conv(_pre(xc, g_pre[1], mod_c, 1), *conv_p)
                xc = _post(xc, yc, g_post[1], mod_c, 1, 1.0)
        else:
            lam_init = 0.8 - 0.6 * math.exp(-0.3 * i)
            yl, yc = _diff_attention(hl, _pre(xc, g_pre[1], mod_c, 1), attn_w_qkv[li],
                                     attn_lambda_q1[li], attn_lambda_k1[li], attn_lambda_q2[li], attn_lambda_k2[li],
                                     attn_subln_g[li], attn_w_o[li], lam_init, rope, not last)
            x = _post(x, yl, g_post[1], mod_l, 1, 1.0)
            if not last:
                xc = _post(xc, yc, g_post[1], mod_c, 1, 1.0)

        x = _half_ffn(x, mod_l, 2, g_pre[2], g_post[2], ffn2_w_in[i], ffn2_w_out[i])
        if not last:
            xc = _half_ffn(xc, mod_c, 2, g_pre[2], g_post[2], ffn2_w_in[i], ffn2_w_out[i])
    return x
```

```python
import functools
import math

import jax
import jax.numpy as jnp
from jax import lax
from jax.experimental import pallas as pl
from jax.experimental.pallas import tpu as pltpu

NORM_EPS = 1e-6
SUBLN_EPS = 1e-5
LN_EPS = 1e-5
GRID_W = 64
HEAD_DIM = 64
HEAD_LANES = 2 * HEAD_DIM
AXIS_DIM = HEAD_DIM // 2
ROPE_THETA = 10000.0
N_SUB = 3
HALO = 16

_BF16 = jnp.bfloat16
_F32 = jnp.float32
_VMEM_LIMIT = 56 * 1024 * 1024


def _cparams(sem):
    return pltpu.CompilerParams(dimension_semantics=sem, vmem_limit_bytes=_VMEM_LIMIT)


def _sigmoid(x):
    return 1.0 / (1.0 + jnp.exp(-x))


def _rms(x, eps):
    return x * lax.rsqrt(jnp.mean(x * x, axis=-1, keepdims=True) + eps)


def _dot(a, b):
    return jnp.dot(a, b, preferred_element_type=_F32)


def _mod_row_map(tm, seq, nb):
    return lambda i, *_: (jnp.minimum((i * tm) // seq, nb), 0, 0)


def _ada_kernel(c_ref, w_ref, b_ref, o_ref):
    c = c_ref[...]
    s = (c * _sigmoid(c)).astype(_BF16)
    o_ref[...] = _dot(s, w_ref[...].astype(_BF16)) + b_ref[...]


def _ada_call(cs, ada_w, ada_b, tn):
    depth, d, n = ada_w.shape
    rows = cs.shape[0]
    return pl.pallas_call(
        _ada_kernel,
        grid=(depth, n // tn),
        in_specs=[
            pl.BlockSpec((rows, d), lambda l, j: (0, 0)),
            pl.BlockSpec((None, d, tn), lambda l, j: (l, 0, j)),
            pl.BlockSpec((None, 1, tn), lambda l, j: (l, 0, j)),
        ],
        out_specs=pl.BlockSpec((None, rows, tn), lambda l, j: (l, 0, j)),
        out_shape=jax.ShapeDtypeStruct((depth, rows, n), _F32),
        compiler_params=_cparams(("parallel", "parallel")),
        name="ada_mod",
    )(cs, ada_w, ada_b.reshape(depth, 1, n))


def _ffn_kernel(x_ref, mod_ref, gpre_ref, gpost_ref, wa_ref, wu_ref, wo_ref, o_ref, h_ref, acc_ref, *, j, nk):
    k = pl.program_id(1)

    @pl.when(k == 0)
    def _():
        x = x_ref[...]
        h = _rms(x, NORM_EPS) * gpre_ref[...] * (1.0 + mod_ref[3 * j + 1:3 * j + 2, :]) + mod_ref[3 * j:3 * j + 1, :]
        h_ref[...] = h.astype(_BF16)
        acc_ref[...] = jnp.zeros_like(acc_ref)

    h = h_ref[...]
    a = _dot(h, wa_ref[...])
    u = _dot(h, wu_ref[...])
    act = (a * _sigmoid(a) * u).astype(_BF16)
    acc_ref[...] += _dot(act, wo_ref[...])

    @pl.when(k == nk - 1)
    def _():
        y = _rms(acc_ref[...], NORM_EPS) * gpost_ref[...]
        o_ref[...] = x_ref[...] + 0.5 * mod_ref[3 * j + 2:3 * j + 3, :] * y


def _ffn_call(xs, mod, g_pre, g_post, w_in, w_out, *, j, n_tok, seq, nb, tm, fc):
    d = xs.shape[1]
    f = w_out.shape[0]
    nk = f // fc
    return pl.pallas_call(
        functools.partial(_ffn_kernel, j=j, nk=nk),
        grid=(n_tok // tm, nk),
        in_specs=[
            pl.BlockSpec((tm, d), lambda i, k: (i, 0)),
            pl.BlockSpec((None, 3 * N_SUB, d), _mod_row_map(tm, seq, nb)),
            pl.BlockSpec((1, d), lambda i, k: (0, 0)),
            pl.BlockSpec((1, d), lambda i, k: (0, 0)),
            pl.BlockSpec((d, fc), lambda i, k: (0, k)),
            pl.BlockSpec((d, fc), lambda i, k: (0, nk + k)),
            pl.BlockSpec((fc, d), lambda i, k: (k, 0)),
        ],
        out_specs=pl.BlockSpec((tm, d), lambda i, k: (i, 0)),
        out_shape=jax.ShapeDtypeStruct((n_tok, d), _F32),
        scratch_shapes=[pltpu.VMEM((tm, d), _BF16), pltpu.VMEM((tm, d), _F32)],
        compiler_params=_cparams(("parallel", "arbitrary")),
        name=f"ffn_half{j}",
    )(xs, mod, g_pre, g_post, w_in, w_in, w_out)


def _pw1_kernel(x_ref, mod_ref, gpre_ref, wa_ref, wg_ref, ba_ref, bg_ref, o_ref, h_ref):
    @pl.when(pl.program_id(1) == 0)
    def _():
        h = _rms(x_ref[...], NORM_EPS) * gpre_ref[...] * (1.0 + mod_ref[4:5, :]) + mod_ref[3:4, :]
        h_ref[...] = h.astype(_BF16)

    h = h_ref[...]
    a = _dot(h, wa_ref[...]) + ba_ref[...]
    g = _dot(h, wg_ref[...]) + bg_ref[...]
    o_ref[...] = (a * _sigmoid(g)).astype(o_ref.dtype)


def _pw1_call(xs, mod, g_pre, w_pw1, b_pw1, *, seq, nb, tm, nc):
    n_tok, d = xs.shape
    nj = d // nc
    return pl.pallas_call(
        _pw1_kernel,
        grid=(n_tok // tm, nj),
        in_specs=[
            pl.BlockSpec((tm, d), lambda i, k: (i, 0)),
            pl.BlockSpec((None, 3 * N_SUB, d), _mod_row_map(tm, seq, nb)),
            pl.BlockSpec((1, d), lambda i, k: (0, 0)),
            pl.BlockSpec((d, nc), lambda i, k: (0, k)),
            pl.BlockSpec((d, nc), lambda i, k: (0, nj + k)),
            pl.BlockSpec((1, nc), lambda i, k: (0, k)),
            pl.BlockSpec((1, nc), lambda i, k: (0, nj + k)),
        ],
        out_specs=pl.BlockSpec((tm, nc), lambda i, k: (i, k)),
        out_shape=jax.ShapeDtypeStruct((n_tok, d), _BF16),
        scratch_shapes=[pltpu.VMEM((tm, d), _BF16)],
        compiler_params=_cparams(("parallel", "arbitrary")),
        name="conv_pw1_glu",
    )(xs, mod, g_pre, w_pw1, w_pw1, b_pw1, b_pw1)


def _dwconv_kernel(u_ref, up_ref, un_ref, x_ref, mod_ref, wdw_ref, bdw_ref, lng_ref, lnb_ref, w2_ref, b2_ref,
                   gpost_ref, o_ref, ext_ref, cv_ref, *, tm, seq, ctx, n_lat, width):
    d = x_ref.shape[1]
    start = pl.program_id(0) * tm
    is_lat = start < n_lat
    pos = jnp.where(is_lat, start % seq, (start - n_lat) % ctx)
    slen = jnp.where(is_lat, seq, ctx)
    keep_prev = (pos != 0).astype(_F32)
    keep_next = (pos + tm != slen).astype(_F32)
    ext_ref[0:HALO, :] = up_ref[...].astype(_F32) * keep_prev
    ext_ref[HALO:HALO + tm, :] = u_ref[...].astype(_F32)
    ext_ref[HALO + tm:2 * HALO + tm, :] = un_ref[...].astype(_F32) * keep_next

    off = HALO - width // 2

    def col_block(c, carry):
        cs = pl.ds(pl.multiple_of(c * 128, 128), 128)
        acc = jnp.zeros((tm, 128), _F32)
        for k in range(width):
            acc = acc + wdw_ref[k:k + 1, cs] * ext_ref[off + k:off + k + tm, cs]
        cv_ref[:, cs] = acc + bdw_ref[:, cs]
        return carry

    lax.fori_loop(0, d // 128, col_block, 0)

    v = cv_ref[...]
    mu = jnp.mean(v, axis=-1, keepdims=True)
    vc = v - mu
    var = jnp.mean(vc * vc, axis=-1, keepdims=True)
    z = vc * lax.rsqrt(var + LN_EPS) * lng_ref[...] + lnb_ref[...]
    z = (z * _sigmoid(z)).astype(_BF16)
    y = _dot(z, w2_ref[...]) + b2_ref[...]
    o_ref[...] = x_ref[...] + mod_ref[5:6, :] * (_rms(y, NORM_EPS) * gpost_ref[...])


def _dwconv_call(u, xs, mod, w_dw, b_dw, ln_g, ln_b, w_pw2, b_pw2, g_post, *, seq, ctx, nb, tm):
    n_tok, d = xs.shape
    width = w_dw.shape[0]
    assert width // 2 <= HALO and tm % HALO == 0 and seq % tm == 0 and ctx % tm == 0
    n_lat = nb * seq
    r = tm // HALO
    last_halo = n_tok // HALO - 1
    wpad = jnp.zeros((-width % 8 + width, d), _F32).at[:width].set(w_dw)
    return pl.pallas_call(
        functools.partial(_dwconv_kernel, tm=tm, seq=seq, ctx=ctx, n_lat=n_lat, width=width),
        grid=(n_tok // tm,),
        in_specs=[
            pl.BlockSpec((tm, d), lambda i: (i, 0)),
            pl.BlockSpec((HALO, d), lambda i: (jnp.maximum(i * r - 1, 0), 0)),
            pl.BlockSpec((HALO, d), lambda i: (jnp.minimum((i + 1) * r, last_halo), 0)),
            pl.BlockSpec((tm, d), lambda i: (i, 0)),
            pl.BlockSpec((None, 3 * N_SUB, d), _mod_row_map(tm, seq, nb)),
            pl.BlockSpec(wpad.shape, lambda i: (0, 0)),
            pl.BlockSpec((1, d), lambda i: (0, 0)),
            pl.BlockSpec((1, d), lambda i: (0, 0)),
            pl.BlockSpec((1, d), lambda i: (0, 0)),
            pl.BlockSpec((d, d), lambda i: (0, 0)),
            pl.BlockSpec((1, d), lambda i: (0, 0)),
            pl.BlockSpec((1, d), lambda i: (0, 0)),
        ],
        out_specs=pl.BlockSpec((tm, d), lambda i: (i, 0)),
        out_shape=jax.ShapeDtypeStruct((n_tok, d), _F32),
        scratch_shapes=[pltpu.VMEM((tm + 2 * HALO, d), _F32), pltpu.VMEM((tm, d), _F32)],
        compiler_params=_cparams(("parallel",)),
        name="conv_dw_pw2",
    )(u, u, u, xs, mod, wpad, b_dw, ln_g, ln_b, w_pw2, b_pw2, g_post)


def _rope_tables(seq, tm):
    t = jnp.arange(seq, dtype=jnp.int32)
    inv_freq = ROPE_THETA ** (-jnp.arange(0, AXIS_DIM, 2, dtype=_F32) / AXIS_DIM)
    parts_c, parts_s = [], []
    for pos in (t // GRID_W, t % GRID_W):
        ang = pos.astype(_F32)[:, None] * inv_freq
        c, s = jnp.cos(ang), jnp.sin(ang)
        parts_c.append(jnp.concatenate([c, c], axis=1))
        parts_s.append(jnp.concatenate([-s, s], axis=1))
    cos = jnp.concatenate(parts_c * 2, axis=1)
    sin = jnp.concatenate(parts_s * 2, axis=1)
    cos = jnp.concatenate([cos, jnp.ones((tm, HEAD_LANES), _F32)], axis=0)
    sin = jnp.concatenate([sin, jnp.zeros((tm, HEAD_LANES), _F32)], axis=0)
    return cos, sin


def _qkv_kernel(x_ref, mod_ref, gpre_ref, w_ref, cos_ref, sin_ref, o_ref, h_ref, *, n_rope, n_q, scale):
    j = pl.program_id(1)

    @pl.when(j == 0)
    def _():
        h = _rms(x_ref[...], NORM_EPS) * gpre_ref[...] * (1.0 + mod_ref[4:5, :]) + mod_ref[3:4, :]
        h_ref[...] = h.astype(_BF16)

    y = _dot(h_ref[...], w_ref[...])

    @pl.when(j >= n_rope)
    def _():
        o_ref[...] = y.astype(o_ref.dtype)

    @pl.when(j < n_rope)
    def _():
        cos = cos_ref[...]
        sin = sin_ref[...]
        first = (lax.broadcasted_iota(jnp.int32, cos.shape, 1) % AXIS_DIM) < AXIS_DIM // 2
        sc = jnp.where(j < n_q, scale, 1.0).astype(_F32)
        for hh in range(y.shape[1] // HEAD_LANES):
            ys = y[:, hh * HEAD_LANES:(hh + 1) * HEAD_LANES]
            partner = jnp.where(first, pltpu.roll(ys, HEAD_LANES - AXIS_DIM // 2, axis=1),
                                pltpu.roll(ys, AXIS_DIM // 2, axis=1))
            r = (ys * cos + partner * sin) * sc
            o_ref[:, hh * HEAD_LANES:(hh + 1) * HEAD_LANES] = r.astype(o_ref.dtype)


def _qkv_call(xs, mod, g_pre, w_qkv, *, seq, nb, tm, nc):
    n_tok, d = xs.shape
    n_lat_tiles = nb * seq // tm
    seq_tiles = seq // tm
    cos, sin = _rope_tables(seq, tm)
    tab_map = lambda i, j: (jnp.where(i < n_lat_tiles, i % seq_tiles, seq_tiles), 0)
    return pl.pallas_call(
        functools.partial(_qkv_kernel, n_rope=2 * d // nc, n_q=d // nc, scale=HEAD_DIM ** -0.5),
        grid=(n_tok // tm, 3 * d // nc),
        in_specs=[
            pl.BlockSpec((tm, d), lambda i, j: (i, 0)),
            pl.BlockSpec((None, 3 * N_SUB, d), _mod_row_map(tm, seq, nb)),
            pl.BlockSpec((1, d), lambda i, j: (0, 0)),
            pl.BlockSpec((d, nc), lambda i, j: (0, j)),
            pl.BlockSpec((tm, HEAD_LANES), tab_map),
            pl.BlockSpec((tm, HEAD_LANES), tab_map),
        ],
        out_specs=pl.BlockSpec((tm, nc), lambda i, j: (i, j)),
        out_shape=jax.ShapeDtypeStruct((n_tok, 3 * d), _BF16),
        scratch_shapes=[pltpu.VMEM((tm, d), _BF16)],
        compiler_params=_cparams(("parallel", "arbitrary")),
        name="attn_qkv_rope",
    )(xs, mod, g_pre, w_qkv, cos, sin)


def _dot_nt(a, b):
    return lax.dot_general(a, b, (((1,), (1,)), ((), ())), preferred_element_type=_F32)


def _attn_kernel(q_ref, kl_ref, kc_ref, vl_ref, vc_ref, lam_ref, g_ref, o_ref, *, lam_init):
    lp = lam_ref[...]
    lam = (jnp.exp(jnp.sum(lp[0:1] * lp[1:2], axis=-1, keepdims=True))
           - jnp.exp(jnp.sum(lp[2:3] * lp[3:4], axis=-1, keepdims=True)) + lam_init)
    q = q_ref[...]
    comp0 = lax.broadcasted_iota(jnp.int32, q.shape, 1) < HEAD_DIM
    zero = jnp.zeros_like(q)
    kl = kl_ref[...]
    kc = kc_ref[...]

    def softmax_parts(qm):
        sl = _dot_nt(qm, kl)
        sc = _dot_nt(qm, kc)
        m = jnp.maximum(jnp.max(sl, axis=-1, keepdims=True), jnp.max(sc, axis=-1, keepdims=True))
        el = jnp.exp(sl - m)
        ec = jnp.exp(sc - m)
        inv = 1.0 / (jnp.sum(el, axis=-1, keepdims=True) + jnp.sum(ec, axis=-1, keepdims=True))
        return el, ec, inv

    e1l, e1c, inv1 = softmax_parts(jnp.where(comp0, q, zero))
    e2l, e2c, inv2 = softmax_parts(jnp.where(comp0, zero, q))
    w2 = lam * inv2
    pl_ = (e1l * inv1 - e2l * w2).astype(_BF16)
    pc_ = (e1c * inv1 - e2c * w2).astype(_BF16)
    o = _dot(pl_, vl_ref[...]) + _dot(pc_, vc_ref[...])
    o = _rms(o, SUBLN_EPS) * g_ref[...] * (1.0 - lam_init)
    o_ref[...] = o.astype(o_ref.dtype)


def _attn_call(qkv, lam_params, subln_g, *, lam_init, seq, ctx, nb, tq):
    d = qkv.shape[1] // 3
    nh = d // HEAD_LANES
    n_lat = nb * seq
    qt = seq // tq
    return pl.pallas_call(
        functools.partial(_attn_kernel, lam_init=lam_init),
        grid=(nb, nh, qt),
        in_specs=[
            pl.BlockSpec((tq, HEAD_LANES), lambda b, h, i: (b * qt + i, h)),
            pl.BlockSpec((seq, HEAD_LANES), lambda b, h, i: (b, nh + h)),
            pl.BlockSpec((ctx, HEAD_LANES), lambda b, h, i: (n_lat // ctx + b, nh + h)),
            pl.BlockSpec((seq, HEAD_LANES), lambda b, h, i: (b, 2 * nh + h)),
            pl.BlockSpec((ctx, HEAD_LANES), lambda b, h, i: (n_lat // ctx + b, 2 * nh + h)),
            pl.BlockSpec((4, HEAD_DIM), lambda b, h, i: (0, 0)),
            pl.BlockSpec((1, HEAD_LANES), lambda b, h, i: (0, 0)),
        ],
        out_specs=pl.BlockSpec((tq, HEAD_LANES), lambda b, h, i: (b * qt + i, h)),
        out_shape=jax.ShapeDtypeStruct((n_lat, d), _BF16),
        compiler_params=_cparams(("parallel", "parallel", "arbitrary")),
        name="diff_attn",
    )(qkv, qkv, qkv, qkv, qkv, lam_params, subln_g)


def _attn_out_kernel(o_ref, x_ref, mod_ref, w_ref, gpost_ref, out_ref):
    y = _dot(o_ref[...], w_ref[...])
    out_ref[...] = x_ref[...] + mod_ref[5:6, :] * (_rms(y, NORM_EPS) * gpost_ref[...])


def _attn_out_call(o, xs, mod, w_o, g_post, *, seq, nb, tm):
    n_lat, d = o.shape
    return pl.pallas_call(
        _attn_out_kernel,
        grid=(n_lat // tm,),
        in_specs=[
            pl.BlockSpec((tm, d), lambda i: (i, 0)),
            pl.BlockSpec((tm, d), lambda i: (i, 0)),
            pl.BlockSpec((None, 3 * N_SUB, d), _mod_row_map(tm, seq, nb)),
            pl.BlockSpec((d, d), lambda i: (0, 0)),
            pl.BlockSpec((1, d), lambda i: (0, 0)),
        ],
        out_specs=pl.BlockSpec((tm, d), lambda i: (i, 0)),
        out_shape=jax.ShapeDtypeStruct((n_lat, d), _F32),
        compiler_params=_cparams(("parallel",)),
        name="attn_out_proj",
    )(o, xs, mod, w_o, g_post)


def _tiles(n_lat_seq, n_ctx_total, d, f):
    tm = min(512, n_lat_seq, n_ctx_total)
    return dict(tm=tm, fc=min(512, f), nc=min(512, d), tconv=min(256, n_lat_seq), tq=min(256, n_lat_seq))


def kernel(x, c, ctx, c_ctx, ada_w, ada_b, norm_pre, norm_post, ffn1_w_in, ffn1_w_out, ffn2_w_in, ffn2_w_out,
           conv_w_pw1, conv_b_pw1, conv_w_dw, conv_b_dw, conv_ln_g, conv_ln_b, conv_w_pw2, conv_b_pw2,
           attn_w_qkv, attn_lambda_q1, attn_lambda_k1, attn_lambda_q2, attn_lambda_k2, attn_subln_g, attn_w_o):
    nb, seq, d = x.shape
    m = ctx.shape[1]
    depth = ada_w.shape[0]
    f = ffn1_w_out.shape[1]
    n_lat, n_ctx = nb * seq, nb * m
    t = _tiles(seq, n_ctx, d, f)
    tm, fc, nc = t["tm"], t["fc"], t["nc"]
    assert seq % tm == 0 and n_ctx % tm == 0 and f % fc == 0 and d % nc == 0 and nc % HEAD_LANES == 0
    assert seq % GRID_W == 0 and d % HEAD_LANES == 0

    bf = lambda w: w.astype(_BF16)
    row = lambda v: v.reshape(1, -1)

    rows = -(-(nb + 1) // 8) * 8
    cs = jnp.zeros((rows, d), _F32).at[:nb].set(c).at[nb].set(c_ctx)
    mods = _ada_call(cs, ada_w, ada_b, min(1024, d)).reshape(depth, rows, 3 * N_SUB, d)

    xs = jnp.concatenate([x.reshape(n_lat, d), ctx.reshape(n_ctx, d)], axis=0)
    n_all = n_lat + n_ctx

    for i in range(depth):
        last = i == depth - 1
        is_conv = i % 2 == 0
        li = i // 2
        mod = mods[i]
        g_pre, g_post = norm_pre[i], norm_post[i]
        ctx_live = not (last and is_conv)
        common = dict(seq=seq, nb=nb)

        n1 = n_all if ctx_live else n_lat
        xs = _ffn_call(xs, mod, row(g_pre[0]), row(g_post[0]), bf(ffn1_w_in[i]), bf(ffn1_w_out[i]),
                       j=0, n_tok=n1, tm=tm, fc=fc, **common)

        if is_conv:
            n2 = n_lat if last else n_all
            if n2 != xs.shape[0]:
                xs = xs[:n2]
            u = _pw1_call(xs, mod, row(g_pre[1]), bf(conv_w_pw1[li]), row(conv_b_pw1[li]), tm=tm, nc=nc, **common)
            xs = _dwconv_call(u, xs, mod, conv_w_dw[li], row(conv_b_dw[li]), row(conv_ln_g[li]), row(conv_ln_b[li]),
                              bf(conv_w_pw2[li]), row(conv_b_pw2[li]), row(g_post[1]), ctx=m, tm=t["tconv"], **common)
        else:
            assert last, "attention layers before the last one are not supported"
            lam_init = 0.8 - 0.6 * math.exp(-0.3 * i)
            qkv = _qkv_call(xs, mod, row(g_pre[1]), bf(attn_w_qkv[li]), tm=tm, nc=nc, **common)
            lam_params = jnp.stack([attn_lambda_q1[li], attn_lambda_k1[li], attn_lambda_q2[li], attn_lambda_k2[li]])
            o = _attn_call(qkv, lam_params, row(attn_subln_g[li]), lam_init=lam_init, seq=seq, ctx=m, nb=nb,
                           tq=t["tq"])
            xs = _attn_out_call(o, xs, mod, bf(attn_w_o[li]), row(g_post[1]), tm=tm, **common)

        n3 = n_lat if last else n_all
        xs = _ffn_call(xs, mod, row(g_pre[2]), row(g_post[2]), bf(ffn2_w_in[i]), bf(ffn2_w_out[i]),
                       j=2, n_tok=n3, tm=tm, fc=fc, **common)

    return xs[:n_lat].reshape(nb, seq, d)
```

```python
import functools
import math

import jax
import jax.numpy as jnp
from jax import lax
from jax.experimental import pallas as pl
from jax.experimental.pallas import tpu as pltpu

NORM_EPS = 1e-6
SUBLN_EPS = 1e-5
LN_EPS = 1e-5
GRID_W = 64
HEAD_DIM = 64
HEAD_LANES = 2 * HEAD_DIM
AXIS_DIM = HEAD_DIM // 2
ROPE_THETA = 10000.0
N_SUB = 3
N_MOD = 3 * N_SUB
LANES = 128
SUBLANES = 8
HALO = 16
ROWS = 16
GS_SLOTS = 3

_BF16 = jnp.bfloat16
_F32 = jnp.float32
_VMEM_LIMIT = 56 * 1024 * 1024


def _cparams(sem):
    return pltpu.CompilerParams(dimension_semantics=sem, vmem_limit_bytes=_VMEM_LIMIT)


def _sigmoid(x):
    return 1.0 / (1.0 + jnp.exp(-x))


def _rms(x, eps):
    return x * lax.rsqrt(jnp.mean(x * x, axis=-1, keepdims=True) + eps)


def _dot(a, b):
    return jnp.dot(a, b, preferred_element_type=_F32)


def _dot_nt(a, b):
    return lax.dot_general(a, b, (((1,), (1,)), ((), ())), preferred_element_type=_F32)


def _mod_map(layer, tm, seq, nb):
    return lambda i, *_: (layer, jnp.minimum((i * tm) // seq, nb), 0, 0)


def _mod_spec(layer, tm, seq, nb, d):
    return pl.BlockSpec((None, None, N_MOD, d), _mod_map(layer, tm, seq, nb))


def _row_spec(index, width):
    return pl.BlockSpec((None, 1, width), lambda *_: (index, 0, 0))


def _set_gains(gs_ref, mod_ref, j, gpre_ref, gpost_ref, weight):
    rep = lambda row: jnp.broadcast_to(row, (SUBLANES, row.shape[1]))
    if gpre_ref is not None:
        gs_ref[0] = rep(gpre_ref[...] * (1.0 + mod_ref[3 * j + 1:3 * j + 2, :]))
        gs_ref[1] = rep(mod_ref[3 * j:3 * j + 1, :])
    if gpost_ref is not None:
        gs_ref[2] = rep((weight * mod_ref[3 * j + 2:3 * j + 3, :]) * gpost_ref[...])


def _row_stats(src_ref, r_ref, eps, mu_ref=None):
    for i in range(src_ref.shape[0] // ROWS):
        rows = slice(i * ROWS, (i + 1) * ROWS)
        v = src_ref[rows, :]
        if mu_ref is not None:
            mu = jnp.mean(v, axis=-1, keepdims=True)
            mu_ref[rows, :] = jnp.broadcast_to(mu, (ROWS, LANES))
            v = v - mu
        r = lax.rsqrt(jnp.mean(v * v, axis=-1, keepdims=True) + eps)
        r_ref[rows, :] = jnp.broadcast_to(r, (ROWS, LANES))


def _lanes(v, width):
    return jnp.concatenate([v] * (width // LANES), axis=1)


def _sublanes(v):
    return jnp.concatenate([v] * (ROWS // SUBLANES), axis=0)


def _row_loop(n_rows, body):
    def step(i, carry):
        body(pl.ds(pl.multiple_of(i * ROWS, ROWS), ROWS))
        return carry

    lax.fori_loop(0, n_rows // ROWS, step, 0, unroll=2)


def _prenorm_rows(x_ref, h_ref, gs_ref, r_ref):
    d = x_ref.shape[1]
    _row_stats(x_ref, r_ref, NORM_EPS)

    def body(rows):
        h = x_ref[rows, :] * _lanes(r_ref[rows, :], d) * _sublanes(gs_ref[0]) + _sublanes(gs_ref[1])
        h_ref[rows, :] = h.astype(h_ref.dtype)

    _row_loop(x_ref.shape[0], body)


def _postnorm_rows(y_ref, x_ref, o_ref, gs_ref, r_ref):
    d = x_ref.shape[1]
    _row_stats(y_ref, r_ref, NORM_EPS)

    def body(rows):
        o_ref[rows, :] = x_ref[rows, :] + y_ref[rows, :] * _lanes(r_ref[rows, :], d) * _sublanes(gs_ref[2])

    _row_loop(x_ref.shape[0], body)


def _ada_kernel(c_ref, w_ref, b_ref, o_ref):
    c = c_ref[...]
    s = (c * _sigmoid(c)).astype(_BF16)
    o_ref[...] = _dot(s, w_ref[...].astype(_BF16)) + b_ref[...]


def _ada_call(cs, ada_w, ada_b, tn):
    depth, d, n = ada_w.shape
    rows = cs.shape[0]
    return pl.pallas_call(
        _ada_kernel,
        grid=(depth, n // tn),
        in_specs=[
            pl.BlockSpec((rows, d), lambda l, j: (0, 0)),
            pl.BlockSpec((None, d, tn), lambda l, j: (l, 0, j)),
            pl.BlockSpec((None, 1, tn), lambda l, j: (l, 0, j)),
        ],
        out_specs=pl.BlockSpec((None, rows, tn), lambda l, j: (l, 0, j)),
        out_shape=jax.ShapeDtypeStruct((depth, rows, n), _F32),
        compiler_params=_cparams(("parallel", "parallel")),
        name="ada_mod",
    )(cs, ada_w, ada_b.reshape(depth, 1, n))


def _ffn_kernel(x_ref, mod_ref, gpre_ref, gpost_ref, wa_ref, wu_ref, wo_ref, o_ref, h_ref, acc_ref, gs_ref, r_ref,
                *, j, nk):
    k = pl.program_id(1)

    @pl.when(k == 0)
    def _():
        _set_gains(gs_ref, mod_ref, j, gpre_ref, gpost_ref, 0.5)
        _prenorm_rows(x_ref, h_ref, gs_ref, r_ref)
        acc_ref[...] = jnp.zeros_like(acc_ref)

    h = h_ref[...]
    a = _dot(h, wa_ref[...])
    u = _dot(h, wu_ref[...])
    act = (a * _sigmoid(a) * u).astype(_BF16)
    acc_ref[...] += _dot(act, wo_ref[...])

    @pl.when(k == nk - 1)
    def _():
        _postnorm_rows(acc_ref, x_ref, o_ref, gs_ref, r_ref)


def _ffn_call(xs, mods, norm_pre, norm_post, w_in, w_out, *, layer, j, n_tok, seq, nb, tm, fc):
    d = xs.shape[1]
    f = w_out.shape[1]
    nk = f // fc
    return pl.pallas_call(
        functools.partial(_ffn_kernel, j=j, nk=nk),
        grid=(n_tok // tm, nk),
        in_specs=[
            pl.BlockSpec((tm, d), lambda i, k: (i, 0)),
            _mod_spec(layer, tm, seq, nb, d),
            _row_spec(layer * N_SUB + j, d),
            _row_spec(layer * N_SUB + j, d),
            pl.BlockSpec((None, d, fc), lambda i, k: (layer, 0, k)),
            pl.BlockSpec((None, d, fc), lambda i, k: (layer, 0, nk + k)),
            pl.BlockSpec((None, fc, d), lambda i, k: (layer, k, 0)),
        ],
        out_specs=pl.BlockSpec((tm, d), lambda i, k: (i, 0)),
        out_shape=jax.ShapeDtypeStruct((n_tok, d), _F32),
        scratch_shapes=[pltpu.VMEM((tm, d), _BF16), pltpu.VMEM((tm, d), _F32), pltpu.VMEM((GS_SLOTS, SUBLANES, d), _F32),
                        pltpu.VMEM((tm, LANES), _F32)],
        compiler_params=_cparams(("parallel", "arbitrary")),
        name=f"ffn_half{j}",
    )(xs, mods, norm_pre, norm_post, w_in, w_in, w_out)


def _pw1_kernel(x_ref, mod_ref, gpre_ref, wa_ref, wg_ref, ba_ref, bg_ref, o_ref, h_ref, gs_ref, r_ref):
    @pl.when(pl.program_id(1) == 0)
    def _():
        _set_gains(gs_ref, mod_ref, 1, gpre_ref, None, 1.0)
        _prenorm_rows(x_ref, h_ref, gs_ref, r_ref)

    h = h_ref[...]
    a = _dot(h, wa_ref[...]) + ba_ref[...]
    g = _dot(h, wg_ref[...]) + bg_ref[...]
    o_ref[...] = (a * _sigmoid(g)).astype(o_ref.dtype)


def _pw1_call(xs, mods, norm_pre, w_pw1, b_pw1, *, layer, li, seq, nb, tm, nc):
    n_tok, d = xs.shape
    nj = d // nc
    return pl.pallas_call(
        _pw1_kernel,
        grid=(n_tok // tm, nj),
        in_specs=[
            pl.BlockSpec((tm, d), lambda i, k: (i, 0)),
            _mod_spec(layer, tm, seq, nb, d),
            _row_spec(layer * N_SUB + 1, d),
            pl.BlockSpec((None, d, nc), lambda i, k: (li, 0, k)),
            pl.BlockSpec((None, d, nc), lambda i, k: (li, 0, nj + k)),
            pl.BlockSpec((None, 1, nc), lambda i, k: (li, 0, k)),
            pl.BlockSpec((None, 1, nc), lambda i, k: (li, 0, nj + k)),
        ],
        out_specs=pl.BlockSpec((tm, nc), lambda i, k: (i, k)),
        out_shape=jax.ShapeDtypeStruct((n_tok, d), _BF16),
        scratch_shapes=[pltpu.VMEM((tm, d), _BF16), pltpu.VMEM((GS_SLOTS, SUBLANES, d), _F32), pltpu.VMEM((tm, LANES), _F32)],
        compiler_params=_cparams(("parallel", "arbitrary")),
        name="conv_pw1_glu",
    )(xs, mods, norm_pre, w_pw1, w_pw1, b_pw1, b_pw1)


def _dwconv_kernel(u_ref, up_ref, un_ref, x_ref, mod_ref, wdw_ref, bdw_ref, lng_ref, lnb_ref, w2_ref, b2_ref,
                   gpost_ref, o_ref, ext_ref, cv_ref, z_ref, gs_ref, r_ref, mu_ref, sh_ref,
                   *, tm, seq, ctx, n_lat, width):
    d = x_ref.shape[1]
    start = pl.program_id(0) * tm
    is_lat = start < n_lat
    pos = jnp.where(is_lat, start % seq, (start - n_lat) % ctx)
    slen = jnp.where(is_lat, seq, ctx)
    keep_prev = (pos != 0).astype(_F32)
    keep_next = (pos + tm != slen).astype(_F32)
    ext_ref[0:HALO, :] = up_ref[...].astype(_F32) * keep_prev
    ext_ref[HALO:HALO + tm, :] = u_ref[...].astype(_F32)
    ext_ref[HALO + tm:2 * HALO + tm, :] = un_ref[...].astype(_F32) * keep_next
    _set_gains(gs_ref, mod_ref, 1, None, gpost_ref, 1.0)

    off = HALO - width // 2
    half = tm // 2
    a_max = (off + width - 1) // SUBLANES * SUBLANES

    def col_block(c, carry):
        cs = pl.ds(pl.multiple_of(c * LANES, LANES), LANES)
        for r0 in (0, half):
            for r in range(1, SUBLANES):
                sh_ref[r, 0:half + a_max, :] = ext_ref[r0 + r:r0 + r + half + a_max, cs]
            acc = jnp.zeros((half, LANES), _F32)
            for k in range(width):
                a, r = (off + k) // SUBLANES * SUBLANES, (off + k) % SUBLANES
                src = ext_ref[r0 + a:r0 + a + half, cs] if r == 0 else sh_ref[r, a:a + half, :]
                acc = acc + wdw_ref[k:k + 1, cs] * src
            cv_ref[r0:r0 + half, cs] = acc + bdw_ref[:, cs]
        return carry

    lax.fori_loop(0, d // LANES, col_block, 0)

    _row_stats(cv_ref, r_ref, LN_EPS, mu_ref)

    def ln_rows(rows):
        z = (cv_ref[rows, :] - _lanes(mu_ref[rows, :], d)) * _lanes(r_ref[rows, :], d) * lng_ref[...] + lnb_ref[...]
        z_ref[rows, :] = (z * _sigmoid(z)).astype(z_ref.dtype)

    _row_loop(tm, ln_rows)
    cv_ref[...] = _dot(z_ref[...], w2_ref[...]) + b2_ref[...]
    _postnorm_rows(cv_ref, x_ref, o_ref, gs_ref, r_ref)


def _dwconv_call(u, xs, mods, norm_post, w_dw, b_dw, ln_g, ln_b, w_pw2, b_pw2, *, layer, li, seq, ctx, nb, tm):
    n_tok, d = xs.shape
    width = w_dw.shape[1]
    assert width // 2 <= HALO and tm % (2 * HALO) == 0 and seq % tm == 0 and ctx % tm == 0
    n_lat = nb * seq
    r = tm // HALO
    last_halo = n_tok // HALO - 1
    wrows = -(-width // SUBLANES) * SUBLANES
    wpad = jnp.zeros((w_dw.shape[0], wrows, d), _F32).at[:, :width].set(w_dw)
    return pl.pallas_call(
        functools.partial(_dwconv_kernel, tm=tm, seq=seq, ctx=ctx, n_lat=n_lat, width=width),
        grid=(n_tok // tm,),
        in_specs=[
            pl.BlockSpec((tm, d), lambda i: (i, 0)),
            pl.BlockSpec((HALO, d), lambda i: (jnp.maximum(i * r - 1, 0), 0)),
            pl.BlockSpec((HALO, d), lambda i: (jnp.minimum((i + 1) * r, last_halo), 0)),
            pl.BlockSpec((tm, d), lambda i: (i, 0)),
            _mod_spec(layer, tm, seq, nb, d),
            pl.BlockSpec((None, wrows, d), lambda i: (li, 0, 0)),
            _row_spec(li, d),
            _row_spec(li, d),
            _row_spec(li, d),
            pl.BlockSpec((None, d, d), lambda i: (li, 0, 0)),
            _row_spec(li, d),
            _row_spec(layer * N_SUB + 1, d),
        ],
        out_specs=pl.BlockSpec((tm, d), lambda i: (i, 0)),
        out_shape=jax.ShapeDtypeStruct((n_tok, d), _F32),
        scratch_shapes=[pltpu.VMEM((tm + 2 * HALO, d), _F32), pltpu.VMEM((tm, d), _F32),
                        pltpu.VMEM((tm, d), _BF16), pltpu.VMEM((GS_SLOTS, SUBLANES, d), _F32),
                        pltpu.VMEM((tm, LANES), _F32), pltpu.VMEM((tm, LANES), _F32),
                        pltpu.VMEM((SUBLANES, tm // 2 + 2 * HALO, LANES), _F32)],
        compiler_params=_cparams(("parallel",)),
        name="conv_dw_pw2",
    )(u, u, u, xs, mods, wpad, b_dw, ln_g, ln_b, w_pw2, b_pw2, norm_post)


def _rope_tables(seq, tm):
    t = jnp.arange(seq, dtype=jnp.int32)
    inv_freq = ROPE_THETA ** (-jnp.arange(0, AXIS_DIM, 2, dtype=_F32) / AXIS_DIM)
    parts_c, parts_s = [], []
    for pos in (t // GRID_W, t % GRID_W):
        ang = pos.astype(_F32)[:, None] * inv_freq
        c, s = jnp.cos(ang), jnp.sin(ang)
        parts_c.append(jnp.concatenate([c, c], axis=1))
        parts_s.append(jnp.concatenate([-s, s], axis=1))
    cos = jnp.concatenate(parts_c * 2, axis=1)
    sin = jnp.concatenate(parts_s * 2, axis=1)
    cos = jnp.concatenate([cos, jnp.ones((tm, HEAD_LANES), _F32)], axis=0)
    sin = jnp.concatenate([sin, jnp.zeros((tm, HEAD_LANES), _F32)], axis=0)
    return cos, sin


def _qkv_kernel(x_ref, mod_ref, gpre_ref, w_ref, cos_ref, sin_ref, o_ref, h_ref, gs_ref, r_ref,
                *, n_rope, n_q, scale):
    j = pl.program_id(1)

    @pl.when(j == 0)
    def _():
        _set_gains(gs_ref, mod_ref, 1, gpre_ref, None, 1.0)
        _prenorm_rows(x_ref, h_ref, gs_ref, r_ref)

    y = _dot(h_ref[...], w_ref[...])

    @pl.when(j >= n_rope)
    def _():
        o_ref[...] = y.astype(o_ref.dtype)

    @pl.when(j < n_rope)
    def _():
        cos = cos_ref[...]
        sin = sin_ref[...]
        first = (lax.broadcasted_iota(jnp.int32, cos.shape, 1) % AXIS_DIM) < AXIS_DIM // 2
        sc = jnp.where(j < n_q, scale, 1.0).astype(_F32)
        for hh in range(y.shape[1] // HEAD_LANES):
            ys = y[:, hh * HEAD_LANES:(hh + 1) * HEAD_LANES]
            partner = jnp.where(first, pltpu.roll(ys, HEAD_LANES - AXIS_DIM // 2, axis=1),
                                pltpu.roll(ys, AXIS_DIM // 2, axis=1))
            r = (ys * cos + partner * sin) * sc
            o_ref[:, hh * HEAD_LANES:(hh + 1) * HEAD_LANES] = r.astype(o_ref.dtype)


def _qkv_call(xs, mods, norm_pre, w_qkv, *, layer, li, seq, nb, tm, nc):
    n_tok, d = xs.shape
    n_lat_tiles = nb * seq // tm
    seq_tiles = seq // tm
    cos, sin = _rope_tables(seq, tm)
    tab_map = lambda i, j: (jnp.where(i < n_lat_tiles, i % seq_tiles, seq_tiles), 0)
    return pl.pallas_call(
        functools.partial(_qkv_kernel, n_rope=2 * d // nc, n_q=d // nc, scale=HEAD_DIM ** -0.5),
        grid=(n_tok // tm, 3 * d // nc),
        in_specs=[
            pl.BlockSpec((tm, d), lambda i, j: (i, 0)),
            _mod_spec(layer, tm, seq, nb, d),
            _row_spec(layer * N_SUB + 1, d),
            pl.BlockSpec((None, d, nc), lambda i, j: (li, 0, j)),
            pl.BlockSpec((tm, HEAD_LANES), tab_map),
            pl.BlockSpec((tm, HEAD_LANES), tab_map),
        ],
        out_specs=pl.BlockSpec((tm, nc), lambda i, j: (i, j)),
        out_shape=jax.ShapeDtypeStruct((n_tok, 3 * d), _BF16),
        scratch_shapes=[pltpu.VMEM((tm, d), _BF16), pltpu.VMEM((GS_SLOTS, SUBLANES, d), _F32), pltpu.VMEM((tm, LANES), _F32)],
        compiler_params=_cparams(("parallel", "arbitrary")),
        name="attn_qkv_rope",
    )(xs, mods, norm_pre, w_qkv, cos, sin)


def _attn_kernel(q_ref, kl_ref, kc_ref, vl_ref, vc_ref, lam_ref, g_ref, o_ref, k_ref, vt_ref,
                 s0_ref, s1_ref, m0_ref, m1_ref, *, lam_init, seq):
    step = pl.program_id(2)

    @pl.when(step == 0)
    def _():
        k_ref[0:seq, :] = kl_ref[...]
        k_ref[seq:, :] = kc_ref[...]
        vt_ref[0:HEAD_LANES, 0:seq] = vl_ref[...].astype(_F32).T.astype(_BF16)
        vt_ref[0:HEAD_LANES, seq:] = vc_ref[...].astype(_F32).T.astype(_BF16)
        vt_ref[HEAD_LANES:, :] = jnp.ones((HALO, vt_ref.shape[1]), _BF16)
        s1_ref[...] = jnp.zeros_like(s1_ref)
        m1_ref[...] = jnp.zeros_like(m1_ref)

    lp = lam_ref[...]
    lam = (jnp.exp(jnp.sum(lp[0:1] * lp[1:2], axis=-1, keepdims=True))
           - jnp.exp(jnp.sum(lp[2:3] * lp[3:4], axis=-1, keepdims=True)) + lam_init)
    tq = q_ref.shape[0]

    def scores(s_ref, m_ref):
        q = q_ref[...]
        comp0 = lax.broadcasted_iota(jnp.int32, q.shape, 1) < HEAD_DIM
        zero = jnp.zeros_like(q)
        q2 = jnp.concatenate([jnp.where(comp0, q, zero), jnp.where(comp0, zero, q)], axis=0)
        st = _dot_nt(k_ref[...], q2)
        s_ref[...] = st
        m_ref[...] = jnp.max(st, axis=0, keepdims=True)

    def finish(s_ref, m_ref):
        e = jnp.exp(s_ref[...] - m_ref[...]).astype(_BF16)
        acc = _dot(vt_ref[...], e)
        w = 1.0 / acc[HEAD_LANES:HEAD_LANES + 1]
        o = (acc[0:HEAD_LANES, 0:tq] * w[:, 0:tq] - acc[0:HEAD_LANES, tq:] * (lam * w[:, tq:])).T
        o = _rms(o, SUBLN_EPS) * g_ref[...] * (1.0 - lam_init)
        o_ref[...] = o.astype(o_ref.dtype)

    even = step % 2 == 0

    @pl.when(even)
    def _():
        finish(s1_ref, m1_ref)
        scores(s0_ref, m0_ref)

    @pl.when(jnp.logical_not(even))
    def _():
        finish(s0_ref, m0_ref)
        scores(s1_ref, m1_ref)


def _attn_call(qkv, lam_params, subln_g, *, li, lam_init, seq, ctx, nb, tq):
    d = qkv.shape[1] // 3
    nh = d // HEAD_LANES
    n_lat = nb * seq
    qt = seq // tq
    keys = seq + ctx
    return pl.pallas_call(
        functools.partial(_attn_kernel, lam_init=lam_init, seq=seq),
        grid=(nb, nh, qt + 1),
        in_specs=[
            pl.BlockSpec((tq, HEAD_LANES), lambda b, h, i: (b * qt + jnp.minimum(i, qt - 1), h)),
            pl.BlockSpec((seq, HEAD_LANES), lambda b, h, i: (b, nh + h)),
            pl.BlockSpec((ctx, HEAD_LANES), lambda b, h, i: (n_lat // ctx + b, nh + h)),
            pl.BlockSpec((seq, HEAD_LANES), lambda b, h, i: (b, 2 * nh + h)),
            pl.BlockSpec((ctx, HEAD_LANES), lambda b, h, i: (n_lat // ctx + b, 2 * nh + h)),
            pl.BlockSpec((None, 4, HEAD_DIM), lambda b, h, i: (li, 0, 0)),
            _row_spec(li, HEAD_LANES),
        ],
        out_specs=pl.BlockSpec((tq, HEAD_LANES), lambda b, h, i: (b * qt + jnp.maximum(i - 1, 0), h)),
        out_shape=jax.ShapeDtypeStruct((n_lat, d), _BF16),
        scratch_shapes=[pltpu.VMEM((keys, HEAD_LANES), _BF16),
                        pltpu.VMEM((HEAD_LANES + HALO, keys), _BF16),
                        pltpu.VMEM((keys, 2 * tq), _F32), pltpu.VMEM((keys, 2 * tq), _F32),
                        pltpu.VMEM((1, 2 * tq), _F32), pltpu.VMEM((1, 2 * tq), _F32)],
        compiler_params=_cparams(("parallel", "parallel", "arbitrary")),
        name="diff_attn",
    )(qkv, qkv, qkv, qkv, qkv, lam_params, subln_g)


def _attn_out_kernel(o_ref, x_ref, mod_ref, w_ref, gpost_ref, out_ref, y_ref, gs_ref, r_ref):
    _set_gains(gs_ref, mod_ref, 1, None, gpost_ref, 1.0)
    y_ref[...] = _dot(o_ref[...], w_ref[...])
    _postnorm_rows(y_ref, x_ref, out_ref, gs_ref, r_ref)


def _attn_out_call(o, xs, mods, norm_post, w_o, *, layer, li, seq, nb, tm):
    n_lat, d = o.shape
    return pl.pallas_call(
        _attn_out_kernel,
        grid=(n_lat // tm,),
        in_specs=[
            pl.BlockSpec((tm, d), lambda i: (i, 0)),
            pl.BlockSpec((tm, d), lambda i: (i, 0)),
            _mod_spec(layer, tm, seq, nb, d),
            pl.BlockSpec((None, d, d), lambda i: (li, 0, 0)),
            _row_spec(layer * N_SUB + 1, d),
        ],
        out_specs=pl.BlockSpec((tm, d), lambda i: (i, 0)),
        out_shape=jax.ShapeDtypeStruct((n_lat, d), _F32),
        scratch_shapes=[pltpu.VMEM((tm, d), _F32), pltpu.VMEM((GS_SLOTS, SUBLANES, d), _F32), pltpu.VMEM((tm, LANES), _F32)],
        compiler_params=_cparams(("parallel",)),
        name="attn_out_proj",
    )(o, xs, mods, w_o, norm_post)


def _tiles(n_lat_seq, n_ctx_total, d, f):
    tm = min(512, n_lat_seq, n_ctx_total)
    return dict(tm=tm, fc=min(512, f), nc=min(512, d), tconv=min(256, n_lat_seq), tq=min(256, n_lat_seq))


def kernel(x, c, ctx, c_ctx, ada_w, ada_b, norm_pre, norm_post, ffn1_w_in, ffn1_w_out, ffn2_w_in, ffn2_w_out,
           conv_w_pw1, conv_b_pw1, conv_w_dw, conv_b_dw, conv_ln_g, conv_ln_b, conv_w_pw2, conv_b_pw2,
           attn_w_qkv, attn_lambda_q1, attn_lambda_k1, attn_lambda_q2, attn_lambda_k2, attn_subln_g, attn_w_o):
    nb, seq, d = x.shape
    m = ctx.shape[1]
    depth = ada_w.shape[0]
    f = ffn1_w_out.shape[1]
    n_lat, n_ctx = nb * seq, nb * m
    t = _tiles(seq, n_ctx, d, f)
    tm, fc, nc = t["tm"], t["fc"], t["nc"]
    assert seq % tm == 0 and n_ctx % tm == 0 and f % fc == 0 and d % nc == 0 and nc % HEAD_LANES == 0
    assert seq % GRID_W == 0 and d % HEAD_LANES == 0

    bf = lambda w: w.astype(_BF16)
    rows3 = lambda v: v.reshape(-1, 1, v.shape[-1])

    rows = -(-(nb + 1) // SUBLANES) * SUBLANES
    cs = jnp.zeros((rows, d), _F32).at[:nb].set(c).at[nb].set(c_ctx)
    mods = _ada_call(cs, ada_w, ada_b, min(1024, d)).reshape(depth, rows, N_MOD, d)

    g_pre, g_post = rows3(norm_pre), rows3(norm_post)
    ffn_w = ((bf(ffn1_w_in), bf(ffn1_w_out)), (bf(ffn2_w_in), bf(ffn2_w_out)))
    w_pw1, w_pw2, w_qkv, w_o = bf(conv_w_pw1), bf(conv_w_pw2), bf(attn_w_qkv), bf(attn_w_o)
    lam_params = jnp.stack([attn_lambda_q1, attn_lambda_k1, attn_lambda_q2, attn_lambda_k2], axis=1)

    xs = jnp.concatenate([x.reshape(n_lat, d), ctx.reshape(n_ctx, d)], axis=0)
    n_all = n_lat + n_ctx

    for i in range(depth):
        last = i == depth - 1
        is_conv = i % 2 == 0
        li = i // 2
        ctx_live = not (last and is_conv)
        common = dict(layer=i, seq=seq, nb=nb)

        xs = _ffn_call(xs, mods, g_pre, g_post, *ffn_w[0], j=0, n_tok=n_all if ctx_live else n_lat,
                       tm=tm, fc=fc, **common)

        if is_conv:
            if last and xs.shape[0] != n_lat:
                xs = xs[:n_lat]
            u = _pw1_call(xs, mods, g_pre, w_pw1, rows3(conv_b_pw1), li=li, tm=tm, nc=nc, **common)
            xs = _dwconv_call(u, xs, mods, g_post, conv_w_dw, rows3(conv_b_dw), rows3(conv_ln_g), rows3(conv_ln_b),
                              w_pw2, rows3(conv_b_pw2), li=li, ctx=m, tm=t["tconv"], **common)
        else:
            assert last, "attention layers before the last one are not supported"
            lam_init = 0.8 - 0.6 * math.exp(-0.3 * i)
            qkv = _qkv_call(xs, mods, g_pre, w_qkv, li=li, tm=tm, nc=nc, **common)
            o = _attn_call(qkv, lam_params, rows3(attn_subln_g), li=li, lam_init=lam_init, seq=seq, ctx=m, nb=nb,
                           tq=t["tq"])
            xs = _attn_out_call(o, xs, mods, g_post, w_o, li=li, tm=tm, **common)

        xs = _ffn_call(xs, mods, g_pre, g_post, *ffn_w[1], j=2, n_tok=n_lat if last else n_all,
                       tm=tm, fc=fc, **common)

    return xs[:n_lat].reshape(nb, seq, d)
```

```python
import functools
import math

import jax
import jax.numpy as jnp
import numpy as np
from jax import lax
from jax.experimental import pallas as pl
from jax.experimental.pallas import tpu as pltpu

NORM_EPS = 1e-6
SUBLN_EPS = 1e-5
LN_EPS = 1e-5
GRID_W = 64
HEAD_DIM = 64
HEAD_LANES = 2 * HEAD_DIM
AXIS_DIM = HEAD_DIM // 2
ROPE_THETA = 10000.0
N_SUB = 3
N_MOD = 3 * N_SUB
LANES = 128
SUBLANES = 8
HALO = 16
ROWS = 16
KEY_CHUNK = 256
GS_SLOTS = 3

_BF16 = jnp.bfloat16
_F32 = jnp.float32
_VMEM_LIMIT = 56 * 1024 * 1024


def _cparams(sem):
    return pltpu.CompilerParams(dimension_semantics=sem, vmem_limit_bytes=_VMEM_LIMIT)


def _sigmoid(x):
    return 1.0 / (1.0 + jnp.exp(-x))


def _rms(x, eps):
    return x * lax.rsqrt(jnp.mean(x * x, axis=-1, keepdims=True) + eps)


def _dot(a, b):
    return jnp.dot(a, b, preferred_element_type=_F32)


def _dot_nt(a, b):
    return lax.dot_general(a, b, (((1,), (1,)), ((), ())), preferred_element_type=_F32)


def _mod_map(layer, tm, seq, nb):
    return lambda i, *_: (layer, jnp.minimum((i * tm) // seq, nb), 0, 0)


def _mod_spec(layer, tm, seq, nb, d):
    return pl.BlockSpec((None, None, N_MOD, d), _mod_map(layer, tm, seq, nb))


def _row_spec(index, width):
    return pl.BlockSpec((None, 1, width), lambda *_: (index, 0, 0))


def _set_gains(gs_ref, mod_ref, j, gpre_ref, gpost_ref, weight):
    rep = lambda row: jnp.broadcast_to(row, (SUBLANES, row.shape[1]))
    if gpre_ref is not None:
        gs_ref[0] = rep(gpre_ref[...] * (1.0 + mod_ref[3 * j + 1:3 * j + 2, :]))
        gs_ref[1] = rep(mod_ref[3 * j:3 * j + 1, :])
    if gpost_ref is not None:
        gs_ref[2] = rep((weight * mod_ref[3 * j + 2:3 * j + 3, :]) * gpost_ref[...])


def _row_stats(src_ref, r_ref, eps, mu_ref=None):
    for i in range(src_ref.shape[0] // ROWS):
        rows = slice(i * ROWS, (i + 1) * ROWS)
        v = src_ref[rows, :]
        if mu_ref is not None:
            mu = jnp.mean(v, axis=-1, keepdims=True)
            mu_ref[rows, :] = jnp.broadcast_to(mu, (ROWS, LANES))
            v = v - mu
        r = lax.rsqrt(jnp.mean(v * v, axis=-1, keepdims=True) + eps)
        r_ref[rows, :] = jnp.broadcast_to(r, (ROWS, LANES))


def _lanes(v, width):
    return jnp.concatenate([v] * (width // LANES), axis=1)


def _sublanes(v):
    return jnp.concatenate([v] * (ROWS // SUBLANES), axis=0)


def _row_loop(n_rows, body):
    def step(i, carry):
        body(pl.ds(pl.multiple_of(i * ROWS, ROWS), ROWS))
        return carry

    lax.fori_loop(0, n_rows // ROWS, step, 0, unroll=2)


def _prenorm_rows(x_ref, h_ref, gs_ref, r_ref):
    d = x_ref.shape[1]
    _row_stats(x_ref, r_ref, NORM_EPS)

    def body(rows):
        h = x_ref[rows, :] * _lanes(r_ref[rows, :], d) * _sublanes(gs_ref[0]) + _sublanes(gs_ref[1])
        h_ref[rows, :] = h.astype(h_ref.dtype)

    _row_loop(x_ref.shape[0], body)


def _postnorm_rows(y_ref, x_ref, o_ref, gs_ref, r_ref):
    d = x_ref.shape[1]
    _row_stats(y_ref, r_ref, NORM_EPS)

    def body(rows):
        o_ref[rows, :] = x_ref[rows, :] + y_ref[rows, :] * _lanes(r_ref[rows, :], d) * _sublanes(gs_ref[2])

    _row_loop(x_ref.shape[0], body)


def _ada_kernel(c_ref, w_ref, b_ref, o_ref):
    c = c_ref[...]
    s = (c * _sigmoid(c)).astype(_BF16)
    o_ref[...] = _dot(s, w_ref[...].astype(_BF16)) + b_ref[...]


def _ada_call(cs, ada_w, ada_b, tn):
    depth, d, n = ada_w.shape
    rows = cs.shape[0]
    return pl.pallas_call(
        _ada_kernel,
        grid=(depth, n // tn),
        in_specs=[
            pl.BlockSpec((rows, d), lambda l, j: (0, 0)),
            pl.BlockSpec((None, d, tn), lambda l, j: (l, 0, j)),
            pl.BlockSpec((None, 1, tn), lambda l, j: (l, 0, j)),
        ],
        out_specs=pl.BlockSpec((None, rows, tn), lambda l, j: (l, 0, j)),
        out_shape=jax.ShapeDtypeStruct((depth, rows, n), _F32),
        compiler_params=_cparams(("parallel", "parallel")),
        name="ada_mod",
    )(cs, ada_w, ada_b.reshape(depth, 1, n))


def _ffn_kernel(*refs, j, nk, n_first):
    sources, refs = refs[:1 if n_first is None else 2], refs[1 if n_first is None else 2:]
    mod_ref, gpre_ref, gpost_ref, wa_ref, wu_ref, wo_ref, o_ref, h_ref, acc_ref, gs_ref, r_ref = refs
    i, k = pl.program_id(0), pl.program_id(1)

    def per_source(fn):
        if n_first is None:
            fn(sources[0])
        else:
            pl.when(i < n_first)(lambda: fn(sources[0]))
            pl.when(i >= n_first)(lambda: fn(sources[1]))

    @pl.when(k == 0)
    def _():
        _set_gains(gs_ref, mod_ref, j, gpre_ref, gpost_ref, 0.5)
        per_source(lambda x_ref: _prenorm_rows(x_ref, h_ref, gs_ref, r_ref))
        acc_ref[...] = jnp.zeros_like(acc_ref)

    h = h_ref[...]
    a = _dot(h, wa_ref[...])
    u = _dot(h, wu_ref[...])
    act = (a * _sigmoid(a) * u).astype(_BF16)
    acc_ref[...] += _dot(act, wo_ref[...])

    @pl.when(k == nk - 1)
    def _():
        per_source(lambda x_ref: _postnorm_rows(acc_ref, x_ref, o_ref, gs_ref, r_ref))


def _ffn_call(xs, mods, norm_pre, norm_post, w_in, w_out, *, layer, j, n_tok, seq, nb, tm, fc, tail=None):
    d = xs.shape[1]
    f = w_out.shape[1]
    nk = f // fc
    n_first = None if tail is None else xs.shape[0] // tm
    if tail is None:
        src_specs, srcs = [pl.BlockSpec((tm, d), lambda i, k: (i, 0))], (xs,)
    else:
        src_specs = [pl.BlockSpec((tm, d), lambda i, k: (jnp.minimum(i, n_first - 1), 0)),
                     pl.BlockSpec((tm, d), lambda i, k: (jnp.maximum(i - n_first, 0), 0))]
        srcs = (xs, tail)
    return pl.pallas_call(
        functools.partial(_ffn_kernel, j=j, nk=nk, n_first=n_first),
        grid=(n_tok // tm, nk),
        in_specs=src_specs + [
            _mod_spec(layer, tm, seq, nb, d),
            _row_spec(layer * N_SUB + j, d),
            _row_spec(layer * N_SUB + j, d),
            pl.BlockSpec((None, d, fc), lambda i, k: (layer, 0, k)),
            pl.BlockSpec((None, d, fc), lambda i, k: (layer, 0, nk + k)),
            pl.BlockSpec((None, fc, d), lambda i, k: (layer, k, 0)),
        ],
        out_specs=pl.BlockSpec((tm, d), lambda i, k: (i, 0)),
        out_shape=jax.ShapeDtypeStruct((n_tok, d), _F32),
        scratch_shapes=[pltpu.VMEM((tm, d), _BF16), pltpu.VMEM((tm, d), _F32), pltpu.VMEM((GS_SLOTS, SUBLANES, d), _F32),
                        pltpu.VMEM((tm, LANES), _F32)],
        compiler_params=_cparams(("parallel", "arbitrary")),
        name=f"ffn_half{j}",
    )(*srcs, mods, norm_pre, norm_post, w_in, w_in, w_out)


def _pw1_kernel(x_ref, mod_ref, gpre_ref, wa_ref, wg_ref, ba_ref, bg_ref, o_ref, h_ref, gs_ref, r_ref):
    @pl.when(pl.program_id(1) == 0)
    def _():
        _set_gains(gs_ref, mod_ref, 1, gpre_ref, None, 1.0)
        _prenorm_rows(x_ref, h_ref, gs_ref, r_ref)

    h = h_ref[...]
    a = _dot(h, wa_ref[...]) + ba_ref[...]
    g = _dot(h, wg_ref[...]) + bg_ref[...]
    o_ref[...] = (a * _sigmoid(g)).astype(o_ref.dtype)


def _pw1_call(xs, mods, norm_pre, w_pw1, b_pw1, *, layer, li, seq, nb, tm, nc):
    n_tok, d = xs.shape
    nj = d // nc
    return pl.pallas_call(
        _pw1_kernel,
        grid=(n_tok // tm, nj),
        in_specs=[
            pl.BlockSpec((tm, d), lambda i, k: (i, 0)),
            _mod_spec(layer, tm, seq, nb, d),
            _row_spec(layer * N_SUB + 1, d),
            pl.BlockSpec((None, d, nc), lambda i, k: (li, 0, k)),
            pl.BlockSpec((None, d, nc), lambda i, k: (li, 0, nj + k)),
            pl.BlockSpec((None, 1, nc), lambda i, k: (li, 0, k)),
            pl.BlockSpec((None, 1, nc), lambda i, k: (li, 0, nj + k)),
        ],
        out_specs=pl.BlockSpec((tm, nc), lambda i, k: (i, k)),
        out_shape=jax.ShapeDtypeStruct((n_tok, d), _BF16),
        scratch_shapes=[pltpu.VMEM((tm, d), _BF16), pltpu.VMEM((GS_SLOTS, SUBLANES, d), _F32), pltpu.VMEM((tm, LANES), _F32)],
        compiler_params=_cparams(("parallel", "arbitrary")),
        name="conv_pw1_glu",
    )(xs, mods, norm_pre, w_pw1, w_pw1, b_pw1, b_pw1)


def _dwconv_kernel(u_ref, up_ref, un_ref, x_ref, mod_ref, wdw_ref, bdw_ref, lng_ref, lnb_ref, w2_ref, b2_ref,
                   gpost_ref, o_ref, ext_ref, cv_ref, z_ref, gs_ref, r_ref, mu_ref, sh_ref,
                   *, tm, seq, ctx, n_lat, width):
    d = x_ref.shape[1]
    start = pl.program_id(0) * tm
    is_lat = start < n_lat
    pos = jnp.where(is_lat, start % seq, (start - n_lat) % ctx)
    slen = jnp.where(is_lat, seq, ctx)
    keep_prev = (pos != 0).astype(_F32)
    keep_next = (pos + tm != slen).astype(_F32)
    ext_ref[0:HALO, :] = up_ref[...].astype(_F32) * keep_prev
    ext_ref[HALO:HALO + tm, :] = u_ref[...].astype(_F32)
    ext_ref[HALO + tm:2 * HALO + tm, :] = un_ref[...].astype(_F32) * keep_next
    _set_gains(gs_ref, mod_ref, 1, None, gpost_ref, 1.0)

    off = HALO - width // 2
    half = tm // 2
    a_max = (off + width - 1) // SUBLANES * SUBLANES

    def col_block(c, carry):
        cs = pl.ds(pl.multiple_of(c * LANES, LANES), LANES)
        for r0 in (0, half):
            for r in range(1, SUBLANES):
                sh_ref[r, 0:half + a_max, :] = ext_ref[r0 + r:r0 + r + half + a_max, cs]
            acc = jnp.zeros((half, LANES), _F32)
            for k in range(width):
                a, r = (off + k) // SUBLANES * SUBLANES, (off + k) % SUBLANES
                src = ext_ref[r0 + a:r0 + a + half, cs] if r == 0 else sh_ref[r, a:a + half, :]
                acc = acc + wdw_ref[k:k + 1, cs] * src
            cv_ref[r0:r0 + half, cs] = acc + bdw_ref[:, cs]
        return carry

    lax.fori_loop(0, d // LANES, col_block, 0)

    _row_stats(cv_ref, r_ref, LN_EPS, mu_ref)

    def ln_rows(rows):
        z = (cv_ref[rows, :] - _lanes(mu_ref[rows, :], d)) * _lanes(r_ref[rows, :], d) * lng_ref[...] + lnb_ref[...]
        z_ref[rows, :] = (z * _sigmoid(z)).astype(z_ref.dtype)

    _row_loop(tm, ln_rows)
    cv_ref[...] = _dot(z_ref[...], w2_ref[...]) + b2_ref[...]
    _postnorm_rows(cv_ref, x_ref, o_ref, gs_ref, r_ref)


def _dwconv_call(u, xs, mods, norm_post, w_dw, b_dw, ln_g, ln_b, w_pw2, b_pw2, *, layer, li, seq, ctx, nb, tm):
    n_tok, d = xs.shape
    width = w_dw.shape[1]
    assert width // 2 <= HALO and tm % (2 * HALO) == 0 and seq % tm == 0 and ctx % tm == 0
    n_lat = nb * seq
    r = tm // HALO
    last_halo = n_tok // HALO - 1
    wrows = -(-width // SUBLANES) * SUBLANES
    wpad = jnp.zeros((w_dw.shape[0], wrows, d), _F32).at[:, :width].set(w_dw)
    return pl.pallas_call(
        functools.partial(_dwconv_kernel, tm=tm, seq=seq, ctx=ctx, n_lat=n_lat, width=width),
        grid=(n_tok // tm,),
        in_specs=[
            pl.BlockSpec((tm, d), lambda i: (i, 0)),
            pl.BlockSpec((HALO, d), lambda i: (jnp.maximum(i * r - 1, 0), 0)),
            pl.BlockSpec((HALO, d), lambda i: (jnp.minimum((i + 1) * r, last_halo), 0)),
            pl.BlockSpec((tm, d), lambda i: (i, 0)),
            _mod_spec(layer, tm, seq, nb, d),
            pl.BlockSpec((None, wrows, d), lambda i: (li, 0, 0)),
            _row_spec(li, d),
            _row_spec(li, d),
            _row_spec(li, d),
            pl.BlockSpec((None, d, d), lambda i: (li, 0, 0)),
            _row_spec(li, d),
            _row_spec(layer * N_SUB + 1, d),
        ],
        out_specs=pl.BlockSpec((tm, d), lambda i: (i, 0)),
        out_shape=jax.ShapeDtypeStruct((n_tok, d), _F32),
        scratch_shapes=[pltpu.VMEM((tm + 2 * HALO, d), _F32), pltpu.VMEM((tm, d), _F32),
                        pltpu.VMEM((tm, d), _BF16), pltpu.VMEM((GS_SLOTS, SUBLANES, d), _F32),
                        pltpu.VMEM((tm, LANES), _F32), pltpu.VMEM((tm, LANES), _F32),
                        pltpu.VMEM((SUBLANES, tm // 2 + 2 * HALO, LANES), _F32)],
        compiler_params=_cparams(("parallel",)),
        name="conv_dw_pw2",
    )(u, u, u, xs, mods, wpad, b_dw, ln_g, ln_b, w_pw2, b_pw2, norm_post)


def _rope_tables(seq, tm):
    t = np.arange(seq, dtype=np.int32)
    inv_freq = (np.float32(ROPE_THETA) ** (-np.arange(0, AXIS_DIM, 2, dtype=np.float32) / np.float32(AXIS_DIM)))
    parts_c, parts_s = [], []
    for pos in (t // GRID_W, t % GRID_W):
        ang = pos.astype(np.float32)[:, None] * inv_freq.astype(np.float32)
        c, s = np.cos(ang).astype(np.float32), np.sin(ang).astype(np.float32)
        parts_c.append(np.concatenate([c, c], axis=1))
        parts_s.append(np.concatenate([-s, s], axis=1))
    cos = np.concatenate(parts_c * 2, axis=1)
    sin = np.concatenate(parts_s * 2, axis=1)
    cos = np.concatenate([cos, np.ones((tm, HEAD_LANES), np.float32)], axis=0)
    sin = np.concatenate([sin, np.zeros((tm, HEAD_LANES), np.float32)], axis=0)
    return jnp.asarray(cos), jnp.asarray(sin)


def _qkv_kernel(x_ref, mod_ref, gpre_ref, w_ref, cos_ref, sin_ref, o_ref, h_ref, gs_ref, r_ref, *, n_q, scale):
    j = pl.program_id(1)

    @pl.when(j == 0)
    def _():
        _set_gains(gs_ref, mod_ref, 1, gpre_ref, None, 1.0)
        _prenorm_rows(x_ref, h_ref, gs_ref, r_ref)

    y = _dot(h_ref[...], w_ref[...])
    cos = cos_ref[...]
    sin = sin_ref[...]
    first = (lax.broadcasted_iota(jnp.int32, cos.shape, 1) % AXIS_DIM) < AXIS_DIM // 2
    sc = jnp.where(j < n_q, scale, 1.0).astype(_F32)
    for hh in range(y.shape[1] // HEAD_LANES):
        ys = y[:, hh * HEAD_LANES:(hh + 1) * HEAD_LANES]
        partner = jnp.where(first, pltpu.roll(ys, HEAD_LANES - AXIS_DIM // 2, axis=1),
                            pltpu.roll(ys, AXIS_DIM // 2, axis=1))
        r = (ys * cos + partner * sin) * sc
        o_ref[:, hh * HEAD_LANES:(hh + 1) * HEAD_LANES] = r.astype(o_ref.dtype)


def _qkv_call(xs, mods, norm_pre, w_qkv, *, layer, li, seq, nb, tm, nc):
    n_tok, d = xs.shape
    n_lat_tiles = nb * seq // tm
    seq_tiles = seq // tm
    cos, sin = _rope_tables(seq, tm)
    n_rope = 2 * d // nc
    tab_map = lambda i, j: (jnp.where((i < n_lat_tiles) & (j < n_rope), i % seq_tiles, seq_tiles), 0)
    return pl.pallas_call(
        functools.partial(_qkv_kernel, n_q=d // nc, scale=HEAD_DIM ** -0.5),
        grid=(n_tok // tm, 3 * d // nc),
        in_specs=[
            pl.BlockSpec((tm, d), lambda i, j: (i, 0)),
            _mod_spec(layer, tm, seq, nb, d),
            _row_spec(layer * N_SUB + 1, d),
            pl.BlockSpec((None, d, nc), lambda i, j: (li, 0, j)),
            pl.BlockSpec((tm, HEAD_LANES), tab_map),
            pl.BlockSpec((tm, HEAD_LANES), tab_map),
        ],
        out_specs=pl.BlockSpec((tm, nc), lambda i, j: (i, j)),
        out_shape=jax.ShapeDtypeStruct((n_tok, 3 * d), _BF16),
        scratch_shapes=[pltpu.VMEM((tm, d), _BF16), pltpu.VMEM((GS_SLOTS, SUBLANES, d), _F32), pltpu.VMEM((tm, LANES), _F32)],
        compiler_params=_cparams(("parallel", "arbitrary")),
        name="attn_qkv_rope",
    )(xs, mods, norm_pre, w_qkv, cos, sin)


def _attn_kernel(q_ref, kl_ref, kc_ref, vl_ref, vc_ref, lam_ref, g_ref, o_ref, k_ref, vt_ref,
                 s0_ref, s1_ref, m0_ref, m1_ref, *, lam_init, seq, n_tiles):
    step = pl.program_id(2)

    @pl.when(step == 0)
    def _():
        k_ref[0:seq, :] = kl_ref[...]
        k_ref[seq:, :] = kc_ref[...]
        vt_ref[0:HEAD_LANES, 0:seq] = vl_ref[...].astype(_F32).T.astype(_BF16)
        vt_ref[0:HEAD_LANES, seq:] = vc_ref[...].astype(_F32).T.astype(_BF16)
        vt_ref[HEAD_LANES:, :] = jnp.ones((HALO, vt_ref.shape[1]), _BF16)

    lp = lam_ref[...]
    lam = (jnp.exp(jnp.sum(lp[0:1] * lp[1:2], axis=-1, keepdims=True))
           - jnp.exp(jnp.sum(lp[2:3] * lp[3:4], axis=-1, keepdims=True)) + lam_init)
    tq = q_ref.shape[0]
    keys = k_ref.shape[0]

    def pipeline_step(new, old):
        if new is not None:
            s_new, m_new = new
            q = q_ref[...]
            comp0 = lax.broadcasted_iota(jnp.int32, q.shape, 1) < HEAD_DIM
            zero = jnp.zeros_like(q)
            q2 = jnp.concatenate([jnp.where(comp0, q, zero), jnp.where(comp0, zero, q)], axis=0)
            q2t = q2.astype(_F32).T.astype(_BF16)
        if old is not None:
            s_old, m_old = old
            m_prev = m_old[...]
        acc = mx = None
        for c in range(keys // KEY_CHUNK):
            rows = slice(c * KEY_CHUNK, (c + 1) * KEY_CHUNK)
            if old is not None:
                e = jnp.exp(s_old[rows, :] - m_prev).astype(_BF16)
                part = _dot(vt_ref[:, rows], e)
                acc = part if acc is None else acc + part
            if new is not None:
                st = _dot(k_ref[rows, :], q2t)
                s_new[rows, :] = st
                cm = jnp.max(st, axis=0, keepdims=True)
                mx = cm if mx is None else jnp.maximum(mx, cm)
        if new is not None:
            m_new[...] = mx
        if old is not None:
            w = 1.0 / acc[HEAD_LANES:HEAD_LANES + 1]
            o = (acc[0:HEAD_LANES, 0:tq] * w[:, 0:tq] - acc[0:HEAD_LANES, tq:] * (lam * w[:, tq:])).T
            o = _rms(o, SUBLN_EPS) * g_ref[...] * (1.0 - lam_init)
            o_ref[...] = o.astype(o_ref.dtype)

    bufs = ((s0_ref, m0_ref), (s1_ref, m1_ref))
    middle = jnp.logical_and(step > 0, step < n_tiles)
    pl.when(step == 0)(lambda: pipeline_step(bufs[0], None))
    pl.when(jnp.logical_and(middle, step % 2 == 0))(lambda: pipeline_step(bufs[0], bufs[1]))
    pl.when(jnp.logical_and(middle, step % 2 == 1))(lambda: pipeline_step(bufs[1], bufs[0]))
    pl.when(step == n_tiles)(lambda: pipeline_step(None, bufs[(n_tiles - 1) % 2]))


def _attn_call(qkv, lam_params, subln_g, *, li, lam_init, seq, ctx, nb, tq):
    d = qkv.shape[1] // 3
    nh = d // HEAD_LANES
    n_lat = nb * seq
    qt = seq // tq
    keys = seq + ctx
    return pl.pallas_call(
        functools.partial(_attn_kernel, lam_init=lam_init, seq=seq, n_tiles=qt),
        grid=(nb, nh, qt + 1),
        in_specs=[
            pl.BlockSpec((tq, HEAD_LANES), lambda b, h, i: (b * qt + jnp.minimum(i, qt - 1), h)),
            pl.BlockSpec((seq, HEAD_LANES), lambda b, h, i: (b, nh + h)),
            pl.BlockSpec((ctx, HEAD_LANES), lambda b, h, i: (n_lat // ctx + b, nh + h)),
            pl.BlockSpec((seq, HEAD_LANES), lambda b, h, i: (b, 2 * nh + h)),
            pl.BlockSpec((ctx, HEAD_LANES), lambda b, h, i: (n_lat // ctx + b, 2 * nh + h)),
            pl.BlockSpec((None, 4, HEAD_DIM), lambda b, h, i: (li, 0, 0)),
            _row_spec(li, HEAD_LANES),
        ],
        out_specs=pl.BlockSpec((tq, HEAD_LANES), lambda b, h, i: (b * qt + jnp.maximum(i - 1, 0), h)),
        out_shape=jax.ShapeDtypeStruct((n_lat, d), _BF16),
        scratch_shapes=[pltpu.VMEM((keys, HEAD_LANES), _BF16),
                        pltpu.VMEM((HEAD_LANES + HALO, keys), _BF16),
                        pltpu.VMEM((keys, 2 * tq), _F32), pltpu.VMEM((keys, 2 * tq), _F32),
                        pltpu.VMEM((1, 2 * tq), _F32), pltpu.VMEM((1, 2 * tq), _F32)],
        compiler_params=_cparams(("parallel", "parallel", "arbitrary")),
        name="diff_attn",
    )(qkv, qkv, qkv, qkv, qkv, lam_params, subln_g)


def _attn_out_kernel(o_ref, x_ref, mod_ref, w_ref, gpost_ref, out_ref, y_ref, gs_ref, r_ref):
    _set_gains(gs_ref, mod_ref, 1, None, gpost_ref, 1.0)
    y_ref[...] = _dot(o_ref[...], w_ref[...])
    _postnorm_rows(y_ref, x_ref, out_ref, gs_ref, r_ref)


def _attn_out_call(o, xs, mods, norm_post, w_o, *, layer, li, seq, nb, tm):
    n_lat, d = o.shape
    return pl.pallas_call(
        _attn_out_kernel,
        grid=(n_lat // tm,),
        in_specs=[
            pl.BlockSpec((tm, d), lambda i: (i, 0)),
            pl.BlockSpec((tm, d), lambda i: (i, 0)),
            _mod_spec(layer, tm, seq, nb, d),
            pl.BlockSpec((None, d, d), lambda i: (li, 0, 0)),
            _row_spec(layer * N_SUB + 1, d),
        ],
        out_specs=pl.BlockSpec((tm, d), lambda i: (i, 0)),
        out_shape=jax.ShapeDtypeStruct((n_lat, d), _F32),
        scratch_shapes=[pltpu.VMEM((tm, d), _F32), pltpu.VMEM((GS_SLOTS, SUBLANES, d), _F32), pltpu.VMEM((tm, LANES), _F32)],
        compiler_params=_cparams(("parallel",)),
        name="attn_out_proj",
    )(o, xs, mods, w_o, norm_post)


def _tiles(n_lat_seq, n_ctx_total, d, f):
    tm = min(512, n_lat_seq, n_ctx_total)
    return dict(tm=tm, fc=min(512, f), nc=min(1024, d), tconv=min(256, n_lat_seq), tq=min(256, n_lat_seq))


def kernel(x, c, ctx, c_ctx, ada_w, ada_b, norm_pre, norm_post, ffn1_w_in, ffn1_w_out, ffn2_w_in, ffn2_w_out,
           conv_w_pw1, conv_b_pw1, conv_w_dw, conv_b_dw, conv_ln_g, conv_ln_b, conv_w_pw2, conv_b_pw2,
           attn_w_qkv, attn_lambda_q1, attn_lambda_k1, attn_lambda_q2, attn_lambda_k2, attn_subln_g, attn_w_o):
    nb, seq, d = x.shape
    m = ctx.shape[1]
    depth = ada_w.shape[0]
    f = ffn1_w_out.shape[1]
    n_lat, n_ctx = nb * seq, nb * m
    t = _tiles(seq, n_ctx, d, f)
    tm, fc, nc = t["tm"], t["fc"], t["nc"]
    assert seq % tm == 0 and n_ctx % tm == 0 and f % fc == 0 and d % nc == 0 and nc % HEAD_LANES == 0
    assert seq % GRID_W == 0 and d % HEAD_LANES == 0

    bf = lambda w: w.astype(_BF16)
    rows3 = lambda v: v.reshape(-1, 1, v.shape[-1])

    rows = -(-(nb + 1) // SUBLANES) * SUBLANES
    cs = jnp.zeros((rows, d), _F32).at[:nb].set(c).at[nb].set(c_ctx)
    mods = _ada_call(cs, ada_w, ada_b, min(1024, d)).reshape(depth, rows, N_MOD, d)

    g_pre, g_post = rows3(norm_pre), rows3(norm_post)
    ffn_w = ((bf(ffn1_w_in), bf(ffn1_w_out)), (bf(ffn2_w_in), bf(ffn2_w_out)))
    w_pw1, w_pw2, w_qkv, w_o = bf(conv_w_pw1), bf(conv_w_pw2), bf(attn_w_qkv), bf(attn_w_o)
    lam_params = jnp.stack([attn_lambda_q1, attn_lambda_k1, attn_lambda_q2, attn_lambda_k2], axis=1)

    xs, tail = x.reshape(n_lat, d), ctx.reshape(n_ctx, d)
    n_all = n_lat + n_ctx

    for i in range(depth):
        last = i == depth - 1
        is_conv = i % 2 == 0
        li = i // 2
        ctx_live = not (last and is_conv)
        common = dict(layer=i, seq=seq, nb=nb)

        xs = _ffn_call(xs, mods, g_pre, g_post, *ffn_w[0], j=0, n_tok=n_all if ctx_live else n_lat,
                       tm=tm, fc=fc, tail=tail if ctx_live else None, **common)
        tail = None

        if is_conv:
            if last and xs.shape[0] != n_lat:
                xs = xs[:n_lat]
            u = _pw1_call(xs, mods, g_pre, w_pw1, rows3(conv_b_pw1), li=li, tm=tm, nc=nc, **common)
            xs = _dwconv_call(u, xs, mods, g_post, conv_w_dw, rows3(conv_b_dw), rows3(conv_ln_g), rows3(conv_ln_b),
                              w_pw2, rows3(conv_b_pw2), li=li, ctx=m, tm=t["tconv"], **common)
        else:
            assert last, "attention layers before the last one are not supported"
            lam_init = 0.8 - 0.6 * math.exp(-0.3 * i)
            qkv = _qkv_call(xs, mods, g_pre, w_qkv, li=li, tm=tm, nc=nc, **common)
            o = _attn_call(qkv, lam_params, rows3(attn_subln_g), li=li, lam_init=lam_init, seq=seq, ctx=m, nb=nb,
                           tq=t["tq"])
            xs = _attn_out_call(o, xs, mods, g_post, w_o, li=li, tm=tm, **common)

        xs = _ffn_call(xs, mods, g_pre, g_post, *ffn_w[1], j=2, n_tok=n_lat if last else n_all,
                       tm=tm, fc=fc, **common)

    return xs[:n_lat].reshape(nb, seq, d)
```

```python
import functools
import math

import jax
import jax.numpy as jnp
import numpy as np
from jax import lax
from jax.experimental import pallas as pl
from jax.experimental.pallas import tpu as pltpu

NORM_EPS = 1e-6
SUBLN_EPS = 1e-5
LN_EPS = 1e-5
GRID_W = 64
HEAD_DIM = 64
HEAD_LANES = 2 * HEAD_DIM
AXIS_DIM = HEAD_DIM // 2
ROPE_THETA = 10000.0
N_SUB = 3
N_MOD = 3 * N_SUB
LANES = 128
SUBLANES = 8
HALO = 16
ROWS = 16
KEY_CHUNK = 256
GS_SLOTS = 3

_BF16 = jnp.bfloat16
_F32 = jnp.float32
_VMEM_LIMIT = 56 * 1024 * 1024


def _cparams(sem):
    return pltpu.CompilerParams(dimension_semantics=sem, vmem_limit_bytes=_VMEM_LIMIT)


def _sigmoid(x):
    return 1.0 / (1.0 + jnp.exp(-x))


def _rms(x, eps):
    return x * lax.rsqrt(jnp.mean(x * x, axis=-1, keepdims=True) + eps)


def _dot(a, b):
    return jnp.dot(a, b, preferred_element_type=_F32)


def _mod_map(layer, tm, seq, nb):
    return lambda i, *_: (layer, jnp.minimum((i * tm) // seq, nb), 0, 0)


def _mod_spec(layer, tm, seq, nb, d):
    return pl.BlockSpec((None, None, N_MOD, d), _mod_map(layer, tm, seq, nb))


def _row_spec(index, width):
    return pl.BlockSpec((None, 1, width), lambda *_: (index, 0, 0))


def _set_gains(gs_ref, mod_ref, j, gpre_ref, gpost_ref, weight):
    rep = lambda row: jnp.broadcast_to(row, (SUBLANES, row.shape[1]))
    if gpre_ref is not None:
        gs_ref[0] = rep(gpre_ref[...] * (1.0 + mod_ref[3 * j + 1:3 * j + 2, :]))
        gs_ref[1] = rep(mod_ref[3 * j:3 * j + 1, :])
    if gpost_ref is not None:
        gs_ref[2] = rep((weight * mod_ref[3 * j + 2:3 * j + 3, :]) * gpost_ref[...])


def _row_stats(src_ref, r_ref, eps, mu_ref=None):
    for i in range(src_ref.shape[0] // ROWS):
        rows = slice(i * ROWS, (i + 1) * ROWS)
        v = src_ref[rows, :]
        if mu_ref is not None:
            mu = jnp.mean(v, axis=-1, keepdims=True)
            mu_ref[rows, :] = jnp.broadcast_to(mu, (ROWS, LANES))
            v = v - mu
        r = lax.rsqrt(jnp.mean(v * v, axis=-1, keepdims=True) + eps)
        r_ref[rows, :] = jnp.broadcast_to(r, (ROWS, LANES))


def _lanes(v, width):
    return jnp.concatenate([v] * (width // LANES), axis=1)


def _sublanes(v):
    return jnp.concatenate([v] * (ROWS // SUBLANES), axis=0)


def _row_loop(n_rows, body):
    def step(i, carry):
        body(pl.ds(pl.multiple_of(i * ROWS, ROWS), ROWS))
        return carry

    lax.fori_loop(0, n_rows // ROWS, step, 0, unroll=2)


def _prenorm_rows(x_ref, h_ref, gs_ref, r_ref):
    d = x_ref.shape[1]
    _row_stats(x_ref, r_ref, NORM_EPS)

    def body(rows):
        h = x_ref[rows, :] * _lanes(r_ref[rows, :], d) * _sublanes(gs_ref[0]) + _sublanes(gs_ref[1])
        h_ref[rows, :] = h.astype(h_ref.dtype)

    _row_loop(x_ref.shape[0], body)


def _postnorm_rows(y_ref, x_ref, o_ref, gs_ref, r_ref):
    d = x_ref.shape[1]
    _row_stats(y_ref, r_ref, NORM_EPS)

    def body(rows):
        o_ref[rows, :] = x_ref[rows, :] + y_ref[rows, :] * _lanes(r_ref[rows, :], d) * _sublanes(gs_ref[2])

    _row_loop(x_ref.shape[0], body)


def _ada_kernel(c_ref, w_ref, b_ref, o_ref):
    c = c_ref[...]
    s = (c * _sigmoid(c)).astype(_BF16)
    o_ref[...] = _dot(s, w_ref[...].astype(_BF16)) + b_ref[...]


def _ada_call(cs, ada_w, ada_b, tn):
    depth, d, n = ada_w.shape
    rows = cs.shape[0]
    return pl.pallas_call(
        _ada_kernel,
        grid=(depth, n // tn),
        in_specs=[
            pl.BlockSpec((rows, d), lambda l, j: (0, 0)),
            pl.BlockSpec((None, d, tn), lambda l, j: (l, 0, j)),
            pl.BlockSpec((None, 1, tn), lambda l, j: (l, 0, j)),
        ],
        out_specs=pl.BlockSpec((None, rows, tn), lambda l, j: (l, 0, j)),
        out_shape=jax.ShapeDtypeStruct((depth, rows, n), _F32),
        compiler_params=_cparams(("parallel", "parallel")),
        name="ada_mod",
    )(cs, ada_w, ada_b.reshape(depth, 1, n))


def _ffn_kernel(*refs, j, nk, n_first):
    sources, refs = refs[:1 if n_first is None else 2], refs[1 if n_first is None else 2:]
    mod_ref, gpre_ref, gpost_ref, wa_ref, wu_ref, wo_ref, o_ref, h_ref, acc_ref, gs_ref, r_ref = refs
    i, k = pl.program_id(0), pl.program_id(1)

    def per_source(fn):
        if n_first is None:
            fn(sources[0])
        else:
            pl.when(i < n_first)(lambda: fn(sources[0]))
            pl.when(i >= n_first)(lambda: fn(sources[1]))

    @pl.when(k == 0)
    def _():
        _set_gains(gs_ref, mod_ref, j, gpre_ref, gpost_ref, 0.5)
        per_source(lambda x_ref: _prenorm_rows(x_ref, h_ref, gs_ref, r_ref))
        acc_ref[...] = jnp.zeros_like(acc_ref)

    h = h_ref[...]
    a = _dot(h, wa_ref[...])
    u = _dot(h, wu_ref[...])
    act = (a * _sigmoid(a) * u).astype(_BF16)
    acc_ref[...] += _dot(act, wo_ref[...])

    @pl.when(k == nk - 1)
    def _():
        per_source(lambda x_ref: _postnorm_rows(acc_ref, x_ref, o_ref, gs_ref, r_ref))


def _ffn_call(xs, mods, norm_pre, norm_post, w_in, w_out, *, layer, j, n_tok, seq, nb, tm, fc, tail=None):
    d = xs.shape[1]
    f = w_out.shape[1]
    nk = f // fc
    n_first = None if tail is None else xs.shape[0] // tm
    if tail is None:
        src_specs, srcs = [pl.BlockSpec((tm, d), lambda i, k: (i, 0))], (xs,)
    else:
        src_specs = [pl.BlockSpec((tm, d), lambda i, k: (jnp.minimum(i, n_first - 1), 0)),
                     pl.BlockSpec((tm, d), lambda i, k: (jnp.maximum(i - n_first, 0), 0),
                                  pipeline_mode=pl.Buffered(1))]
        srcs = (xs, tail)
    return pl.pallas_call(
        functools.partial(_ffn_kernel, j=j, nk=nk, n_first=n_first),
        grid=(n_tok // tm, nk),
        in_specs=src_specs + [
            _mod_spec(layer, tm, seq, nb, d),
            _row_spec(layer * N_SUB + j, d),
            _row_spec(layer * N_SUB + j, d),
            pl.BlockSpec((None, d, fc), lambda i, k: (layer, 0, k)),
            pl.BlockSpec((None, d, fc), lambda i, k: (layer, 0, nk + k)),
            pl.BlockSpec((None, fc, d), lambda i, k: (layer, k, 0)),
        ],
        out_specs=pl.BlockSpec((tm, d), lambda i, k: (i, 0)),
        out_shape=jax.ShapeDtypeStruct((n_tok, d), _F32),
        scratch_shapes=[pltpu.VMEM((tm, d), _BF16), pltpu.VMEM((tm, d), _F32), pltpu.VMEM((GS_SLOTS, SUBLANES, d), _F32),
                        pltpu.VMEM((tm, LANES), _F32)],
        compiler_params=_cparams(("parallel", "arbitrary")),
        name=f"ffn_half{j}",
    )(*srcs, mods, norm_pre, norm_post, w_in, w_in, w_out)


def _pw1_kernel(x_ref, mod_ref, gpre_ref, wa_ref, wg_ref, ba_ref, bg_ref, o_ref, h_ref, gs_ref, r_ref):
    @pl.when(pl.program_id(1) == 0)
    def _():
        _set_gains(gs_ref, mod_ref, 1, gpre_ref, None, 1.0)
        _prenorm_rows(x_ref, h_ref, gs_ref, r_ref)

    h = h_ref[...]
    a = _dot(h, wa_ref[...]) + ba_ref[...]
    g = _dot(h, wg_ref[...]) + bg_ref[...]
    o_ref[...] = (a * _sigmoid(g)).astype(o_ref.dtype)


def _pw1_call(xs, mods, norm_pre, w_pw1, b_pw1, *, layer, li, seq, nb, tm, nc):
    n_tok, d = xs.shape
    nj = d // nc
    return pl.pallas_call(
        _pw1_kernel,
        grid=(n_tok // tm, nj),
        in_specs=[
            pl.BlockSpec((tm, d), lambda i, k: (i, 0)),
            _mod_spec(layer, tm, seq, nb, d),
            _row_spec(layer * N_SUB + 1, d),
            pl.BlockSpec((None, d, nc), lambda i, k: (li, 0, k)),
            pl.BlockSpec((None, d, nc), lambda i, k: (li, 0, nj + k)),
            pl.BlockSpec((None, 1, nc), lambda i, k: (li, 0, k)),
            pl.BlockSpec((None, 1, nc), lambda i, k: (li, 0, nj + k)),
        ],
        out_specs=pl.BlockSpec((tm, nc), lambda i, k: (i, k)),
        out_shape=jax.ShapeDtypeStruct((n_tok, d), _BF16),
        scratch_shapes=[pltpu.VMEM((tm, d), _BF16), pltpu.VMEM((GS_SLOTS, SUBLANES, d), _F32), pltpu.VMEM((tm, LANES), _F32)],
        compiler_params=_cparams(("parallel", "arbitrary")),
        name="conv_pw1_glu",
    )(xs, mods, norm_pre, w_pw1, w_pw1, b_pw1, b_pw1)


def _dwconv_kernel(u_ref, up_ref, un_ref, x_ref, mod_ref, wdw_ref, bdw_ref, lng_ref, lnb_ref, w2_ref, b2_ref,
                   gpost_ref, o_ref, ext_ref, cv_ref, z_ref, gs_ref, r_ref, mu_ref, sh_ref,
                   *, tm, seq, ctx, n_lat, width):
    d = x_ref.shape[1]
    start = pl.program_id(0) * tm
    is_lat = start < n_lat
    pos = jnp.where(is_lat, start % seq, (start - n_lat) % ctx)
    slen = jnp.where(is_lat, seq, ctx)
    keep_prev = (pos != 0).astype(_F32)
    keep_next = (pos + tm != slen).astype(_F32)
    ext_ref[0:HALO, :] = up_ref[...].astype(_F32) * keep_prev
    ext_ref[HALO:HALO + tm, :] = u_ref[...].astype(_F32)
    ext_ref[HALO + tm:2 * HALO + tm, :] = un_ref[...].astype(_F32) * keep_next
    _set_gains(gs_ref, mod_ref, 1, None, gpost_ref, 1.0)

    off = HALO - width // 2
    half = tm // 2
    a_max = (off + width - 1) // SUBLANES * SUBLANES

    def col_block(c, carry):
        cs = pl.ds(pl.multiple_of(c * LANES, LANES), LANES)
        for r0 in (0, half):
            for r in range(1, SUBLANES):
                sh_ref[r, 0:half + a_max, :] = ext_ref[r0 + r:r0 + r + half + a_max, cs]
            acc = jnp.zeros((half, LANES), _F32)
            for r in range(SUBLANES):
                taps = [k for k in range(width) if (off + k) % SUBLANES == r]
                if not taps:
                    continue
                rows_r = ext_ref[r0:r0 + half + a_max, cs] if r == 0 else sh_ref[r, 0:half + a_max, :]
                for k in taps:
                    a = (off + k) // SUBLANES * SUBLANES
                    acc = acc + wdw_ref[k:k + 1, cs] * rows_r[a:a + half, :]
            cv_ref[r0:r0 + half, cs] = acc + bdw_ref[:, cs]
        return carry

    lax.fori_loop(0, d // LANES, col_block, 0)

    _row_stats(cv_ref, r_ref, LN_EPS, mu_ref)

    def ln_rows(rows):
        z = (cv_ref[rows, :] - _lanes(mu_ref[rows, :], d)) * _lanes(r_ref[rows, :], d) * lng_ref[...] + lnb_ref[...]
        z_ref[rows, :] = (z * _sigmoid(z)).astype(z_ref.dtype)

    _row_loop(tm, ln_rows)
    cv_ref[...] = _dot(z_ref[...], w2_ref[...]) + b2_ref[...]
    _postnorm_rows(cv_ref, x_ref, o_ref, gs_ref, r_ref)


def _dwconv_call(u, xs, mods, norm_post, w_dw, b_dw, ln_g, ln_b, w_pw2, b_pw2, *, layer, li, seq, ctx, nb, tm):
    n_tok, d = xs.shape
    width = w_dw.shape[1]
    assert width // 2 <= HALO and tm % (2 * HALO) == 0 and seq % tm == 0 and ctx % tm == 0
    n_lat = nb * seq
    r = tm // HALO
    last_halo = n_tok // HALO - 1
    wrows = -(-width // SUBLANES) * SUBLANES
    wpad = jnp.zeros((w_dw.shape[0], wrows, d), _F32).at[:, :width].set(w_dw)
    return pl.pallas_call(
        functools.partial(_dwconv_kernel, tm=tm, seq=seq, ctx=ctx, n_lat=n_lat, width=width),
        grid=(n_tok // tm,),
        in_specs=[
            pl.BlockSpec((tm, d), lambda i: (i, 0)),
            pl.BlockSpec((HALO, d), lambda i: (jnp.maximum(i * r - 1, 0), 0)),
            pl.BlockSpec((HALO, d), lambda i: (jnp.minimum((i + 1) * r, last_halo), 0)),
            pl.BlockSpec((tm, d), lambda i: (i, 0)),
            _mod_spec(layer, tm, seq, nb, d),
            pl.BlockSpec((None, wrows, d), lambda i: (li, 0, 0)),
            _row_spec(li, d),
            _row_spec(li, d),
            _row_spec(li, d),
            pl.BlockSpec((None, d, d), lambda i: (li, 0, 0)),
            _row_spec(li, d),
            _row_spec(layer * N_SUB + 1, d),
        ],
        out_specs=pl.BlockSpec((tm, d), lambda i: (i, 0)),
        out_shape=jax.ShapeDtypeStruct((n_tok, d), _F32),
        scratch_shapes=[pltpu.VMEM((tm + 2 * HALO, d), _F32), pltpu.VMEM((tm, d), _F32),
                        pltpu.VMEM((tm, d), _BF16), pltpu.VMEM((GS_SLOTS, SUBLANES, d), _F32),
                        pltpu.VMEM((tm, LANES), _F32), pltpu.VMEM((tm, LANES), _F32),
                        pltpu.VMEM((SUBLANES, tm // 2 + 2 * HALO, LANES), _F32)],
        compiler_params=_cparams(("parallel",)),
        name="conv_dw_pw2",
    )(u, u, u, xs, mods, wpad, b_dw, ln_g, ln_b, w_pw2, b_pw2, norm_post)


def _rope_tables(seq, tm):
    t = np.arange(seq, dtype=np.int32)
    inv_freq = (np.float32(ROPE_THETA) ** (-np.arange(0, AXIS_DIM, 2, dtype=np.float32) / np.float32(AXIS_DIM)))
    parts_c, parts_s = [], []
    for pos in (t // GRID_W, t % GRID_W):
        ang = pos.astype(np.float32)[:, None] * inv_freq.astype(np.float32)
        parts_c.append(np.cos(ang).astype(np.float32))
        parts_s.append(np.sin(ang).astype(np.float32))
    cos = np.concatenate(parts_c * 4, axis=1)
    sin = np.concatenate([-s for s in parts_s] * 2 + parts_s * 2, axis=1)
    cos = np.concatenate([cos, np.ones((tm, HEAD_LANES), np.float32)], axis=0)
    sin = np.concatenate([sin, np.zeros((tm, HEAD_LANES), np.float32)], axis=0)
    return jnp.asarray(cos), jnp.asarray(sin)


def _rope_layout(w_qkv):
    layers, d, _ = w_qkv.shape
    half = AXIS_DIM // 2
    qk = w_qkv[:, :, :2 * d].reshape(layers, d, -1, HEAD_LANES // AXIS_DIM, 2, half)
    qk = qk.transpose(0, 1, 2, 4, 3, 5).reshape(layers, d, 2 * d)
    return jnp.concatenate([qk, w_qkv[:, :, 2 * d:]], axis=-1)


def _qkv_kernel(x_ref, mod_ref, gpre_ref, w_ref, cos_ref, sin_ref, o_ref, h_ref, gs_ref, r_ref, *, n_q, scale):
    j = pl.program_id(1)

    @pl.when(j == 0)
    def _():
        _set_gains(gs_ref, mod_ref, 1, gpre_ref, None, 1.0)
        _prenorm_rows(x_ref, h_ref, gs_ref, r_ref)

    y = _dot(h_ref[...], w_ref[...])
    cos = cos_ref[...]
    sin = sin_ref[...]
    sc = jnp.where(j < n_q, scale, 1.0).astype(_F32)
    for hh in range(y.shape[1] // HEAD_LANES):
        ys = y[:, hh * HEAD_LANES:(hh + 1) * HEAD_LANES]
        partner = pltpu.roll(ys, HEAD_LANES // 2, axis=1)
        r = (ys * cos + partner * sin) * sc
        o_ref[:, hh * HEAD_LANES:(hh + 1) * HEAD_LANES] = r.astype(o_ref.dtype)


def _qkv_call(xs, mods, norm_pre, w_qkv, *, layer, li, seq, nb, tm, nc):
    n_tok, d = xs.shape
    n_lat_tiles = nb * seq // tm
    seq_tiles = seq // tm
    cos, sin = _rope_tables(seq, tm)
    n_rope = 2 * d // nc
    tab_map = lambda i, j: (jnp.where((i < n_lat_tiles) & (j < n_rope), i % seq_tiles, seq_tiles), 0)
    return pl.pallas_call(
        functools.partial(_qkv_kernel, n_q=d // nc, scale=HEAD_DIM ** -0.5 * math.log2(math.e)),
        grid=(n_tok // tm, 3 * d // nc),
        in_specs=[
            pl.BlockSpec((tm, d), lambda i, j: (i, 0)),
            _mod_spec(layer, tm, seq, nb, d),
            _row_spec(layer * N_SUB + 1, d),
            pl.BlockSpec((None, d, nc), lambda i, j: (li, 0, j)),
            pl.BlockSpec((tm, HEAD_LANES), tab_map),
            pl.BlockSpec((tm, HEAD_LANES), tab_map),
        ],
        out_specs=pl.BlockSpec((tm, nc), lambda i, j: (i, j)),
        out_shape=jax.ShapeDtypeStruct((n_tok, 3 * d), _BF16),
        scratch_shapes=[pltpu.VMEM((tm, d), _BF16), pltpu.VMEM((GS_SLOTS, SUBLANES, d), _F32), pltpu.VMEM((tm, LANES), _F32)],
        compiler_params=_cparams(("parallel", "arbitrary")),
        name="attn_qkv_rope",
    )(xs, mods, norm_pre, w_qkv, cos, sin)


def _attn_kernel(q_ref, kl_ref, kc_ref, vl_ref, vc_ref, lam_ref, g_ref, o_ref, k_ref, vt_ref,
                 s0_ref, s1_ref, m0_ref, m1_ref, *, lam_init, seq, n_tiles):
    step = pl.program_id(2)

    @pl.when(step == 0)
    def _():
        k_ref[0:seq, :] = kl_ref[...]
        k_ref[seq:, :] = kc_ref[...]
        vt_ref[0:HEAD_LANES, 0:seq] = vl_ref[...].astype(_F32).T.astype(_BF16)
        vt_ref[0:HEAD_LANES, seq:] = vc_ref[...].astype(_F32).T.astype(_BF16)

    lp = lam_ref[...]
    lam = (jnp.exp(jnp.sum(lp[0:1] * lp[1:2], axis=-1, keepdims=True))
           - jnp.exp(jnp.sum(lp[2:3] * lp[3:4], axis=-1, keepdims=True)) + lam_init)
    tq = q_ref.shape[0]
    keys = k_ref.shape[0]

    def pipeline_step(new, old):
        if new is not None:
            s_new, m_new = new
            q = q_ref[...]
            comp0 = lax.broadcasted_iota(jnp.int32, q.shape, 1) % HEAD_DIM < HEAD_DIM // 2
            zero = jnp.zeros_like(q)
            q2 = jnp.concatenate([jnp.where(comp0, q, zero), jnp.where(comp0, zero, q)], axis=0)
            q2t = q2.astype(_F32).T.astype(_BF16)
        if old is not None:
            s_old, m_old = old
            m_prev = m_old[...]
        acc = den = mx = None
        for c in range(keys // KEY_CHUNK):
            rows = slice(c * KEY_CHUNK, (c + 1) * KEY_CHUNK)
            if old is not None:
                e = jnp.exp2(s_old[rows, :] - m_prev)
                ls = jnp.sum(e, axis=0, keepdims=True)
                den = ls if den is None else den + ls
                part = _dot(vt_ref[:, rows], e.astype(_BF16))
                acc = part if acc is None else acc + part
            if new is not None:
                st = _dot(k_ref[rows, :], q2t)
                s_new[rows, :] = st
                cm = jnp.max(st, axis=0, keepdims=True)
                mx = cm if mx is None else jnp.maximum(mx, cm)
        if new is not None:
            m_new[...] = mx
        if old is not None:
            w = 1.0 / den
            o = (acc[:, 0:tq] * w[:, 0:tq] - acc[:, tq:] * (lam * w[:, tq:])).T
            o = _rms(o, SUBLN_EPS) * g_ref[...] * (1.0 - lam_init)
            o_ref[...] = o.astype(o_ref.dtype)

    bufs = ((s0_ref, m0_ref), (s1_ref, m1_ref))
    middle = jnp.logical_and(step > 0, step < n_tiles)
    pl.when(step == 0)(lambda: pipeline_step(bufs[0], None))
    pl.when(jnp.logical_and(middle, step % 2 == 0))(lambda: pipeline_step(bufs[0], bufs[1]))
    pl.when(jnp.logical_and(middle, step % 2 == 1))(lambda: pipeline_step(bufs[1], bufs[0]))
    pl.when(step == n_tiles)(lambda: pipeline_step(None, bufs[(n_tiles - 1) % 2]))


def _attn_call(qkv, lam_params, subln_g, *, li, lam_init, seq, ctx, nb, tq):
    d = qkv.shape[1] // 3
    nh = d // HEAD_LANES
    n_lat = nb * seq
    qt = seq // tq
    keys = seq + ctx
    return pl.pallas_call(
        functools.partial(_attn_kernel, lam_init=lam_init, seq=seq, n_tiles=qt),
        grid=(nb, nh, qt + 1),
        in_specs=[
            pl.BlockSpec((tq, HEAD_LANES), lambda b, h, i: (b * qt + jnp.minimum(i, qt - 1), h)),
            pl.BlockSpec((seq, HEAD_LANES), lambda b, h, i: (b, nh + h)),
            pl.BlockSpec((ctx, HEAD_LANES), lambda b, h, i: (n_lat // ctx + b, nh + h)),
            pl.BlockSpec((seq, HEAD_LANES), lambda b, h, i: (b, 2 * nh + h)),
            pl.BlockSpec((ctx, HEAD_LANES), lambda b, h, i: (n_lat // ctx + b, 2 * nh + h)),
            pl.BlockSpec((None, 4, HEAD_DIM), lambda b, h, i: (li, 0, 0)),
            _row_spec(li, HEAD_LANES),
        ],
        out_specs=pl.BlockSpec((tq, HEAD_LANES), lambda b, h, i: (b * qt + jnp.maximum(i - 1, 0), h)),
        out_shape=jax.ShapeDtypeStruct((n_lat, d), _BF16),
        scratch_shapes=[pltpu.VMEM((keys, HEAD_LANES), _BF16),
                        pltpu.VMEM((HEAD_LANES, keys), _BF16),
                        pltpu.VMEM((keys, 2 * tq), _F32), pltpu.VMEM((keys, 2 * tq), _F32),
                        pltpu.VMEM((1, 2 * tq), _F32), pltpu.VMEM((1, 2 * tq), _F32)],
        compiler_params=_cparams(("parallel", "parallel", "arbitrary")),
        name="diff_attn",
    )(qkv, qkv, qkv, qkv, qkv, lam_params, subln_g)


def _attn_out_kernel(o_ref, x_ref, mod_ref, w_ref, gpost_ref, out_ref, y_ref, gs_ref, r_ref):
    _set_gains(gs_ref, mod_ref, 1, None, gpost_ref, 1.0)
    y_ref[...] = _dot(o_ref[...], w_ref[...])
    _postnorm_rows(y_ref, x_ref, out_ref, gs_ref, r_ref)


def _attn_out_call(o, xs, mods, norm_post, w_o, *, layer, li, seq, nb, tm):
    n_lat, d = o.shape
    return pl.pallas_call(
        _attn_out_kernel,
        grid=(n_lat // tm,),
        in_specs=[
            pl.BlockSpec((tm, d), lambda i: (i, 0)),
            pl.BlockSpec((tm, d), lambda i: (i, 0)),
            _mod_spec(layer, tm, seq, nb, d),
            pl.BlockSpec((None, d, d), lambda i: (li, 0, 0)),
            _row_spec(layer * N_SUB + 1, d),
        ],
        out_specs=pl.BlockSpec((tm, d), lambda i: (i, 0)),
        out_shape=jax.ShapeDtypeStruct((n_lat, d), _F32),
        scratch_shapes=[pltpu.VMEM((tm, d), _F32), pltpu.VMEM((GS_SLOTS, SUBLANES, d), _F32), pltpu.VMEM((tm, LANES), _F32)],
        compiler_params=_cparams(("parallel",)),
        name="attn_out_proj",
    )(o, xs, mods, w_o, norm_post)


def _tiles(n_lat_seq, n_ctx_total, d, f):
    tm = min(512, n_lat_seq, n_ctx_total)
    return dict(tm=tm, fc=min(512, f), nc=min(1024, d), nc_qkv=min(2048, d), tconv=min(256, n_lat_seq),
                tq=min(256, n_lat_seq))


def kernel(x, c, ctx, c_ctx, ada_w, ada_b, norm_pre, norm_post, ffn1_w_in, ffn1_w_out, ffn2_w_in, ffn2_w_out,
           conv_w_pw1, conv_b_pw1, conv_w_dw, conv_b_dw, conv_ln_g, conv_ln_b, conv_w_pw2, conv_b_pw2,
           attn_w_qkv, attn_lambda_q1, attn_lambda_k1, attn_lambda_q2, attn_lambda_k2, attn_subln_g, attn_w_o):
    nb, seq, d = x.shape
    m = ctx.shape[1]
    depth = ada_w.shape[0]
    f = ffn1_w_out.shape[1]
    n_lat, n_ctx = nb * seq, nb * m
    t = _tiles(seq, n_ctx, d, f)
    tm, fc, nc = t["tm"], t["fc"], t["nc"]
    assert seq % tm == 0 and n_ctx % tm == 0 and f % fc == 0 and d % nc == 0 and nc % HEAD_LANES == 0
    assert seq % GRID_W == 0 and d % HEAD_LANES == 0

    bf = lambda w: w.astype(_BF16)
    rows3 = lambda v: v.reshape(-1, 1, v.shape[-1])

    rows = -(-(nb + 1) // SUBLANES) * SUBLANES
    cs = jnp.zeros((rows, d), _F32).at[:nb].set(c).at[nb].set(c_ctx)
    mods = _ada_call(cs, ada_w, ada_b, min(1024, d)).reshape(depth, rows, N_MOD, d)

    g_pre, g_post = rows3(norm_pre), rows3(norm_post)
    ffn_w = ((bf(ffn1_w_in), bf(ffn1_w_out)), (bf(ffn2_w_in), bf(ffn2_w_out)))
    w_pw1, w_pw2, w_qkv, w_o = bf(conv_w_pw1), bf(conv_w_pw2), _rope_layout(bf(attn_w_qkv)), bf(attn_w_o)
    lam_params = jnp.stack([attn_lambda_q1, attn_lambda_k1, attn_lambda_q2, attn_lambda_k2], axis=1)

    xs, tail = x.reshape(n_lat, d), ctx.reshape(n_ctx, d)
    n_all = n_lat + n_ctx

    for i in range(depth):
        last = i == depth - 1
        is_conv = i % 2 == 0
        li = i // 2
        ctx_live = not (last and is_conv)
        common = dict(layer=i, seq=seq, nb=nb)

        xs = _ffn_call(xs, mods, g_pre, g_post, *ffn_w[0], j=0, n_tok=n_all if ctx_live else n_lat,
                       tm=tm, fc=fc, tail=tail if ctx_live else None, **common)
        tail = None

        if is_conv:
            if last and xs.shape[0] != n_lat:
                xs = xs[:n_lat]
            u = _pw1_call(xs, mods, g_pre, w_pw1, rows3(conv_b_pw1), li=li, tm=tm, nc=nc, **common)
            xs = _dwconv_call(u, xs, mods, g_post, conv_w_dw, rows3(conv_b_dw), rows3(conv_ln_g), rows3(conv_ln_b),
                              w_pw2, rows3(conv_b_pw2), li=li, ctx=m, tm=t["tconv"], **common)
        else:
            assert last, "attention layers before the last one are not supported"
            lam_init = 0.8 - 0.6 * math.exp(-0.3 * i)
            qkv = _qkv_call(xs, mods, g_pre, w_qkv, li=li, tm=tm, nc=t["nc_qkv"], **common)
            o = _attn_call(qkv, lam_params, rows3(attn_subln_g), li=li, lam_init=lam_init, seq=seq, ctx=m, nb=nb,
                           tq=t["tq"])
            xs = _attn_out_call(o, xs, mods, g_post, w_o, li=li, tm=tm, **common)

        xs = _ffn_call(xs, mods, g_pre, g_post, *ffn_w[1], j=2, n_tok=n_lat if last else n_all,
                       tm=tm, fc=fc, **common)

    return xs[:n_lat].reshape(nb, seq, d)
```

```python
import functools
import math

import jax
import jax.numpy as jnp
import numpy as np
from jax import lax
from jax.experimental import pallas as pl
from jax.experimental.pallas import tpu as pltpu

NORM_EPS = 1e-6
SUBLN_EPS = 1e-5
LN_EPS = 1e-5
GRID_W = 64
HEAD_DIM = 64
HEAD_LANES = 2 * HEAD_DIM
AXIS_DIM = HEAD_DIM // 2
ROPE_THETA = 10000.0
N_SUB = 3
N_MOD = 3 * N_SUB
LANES = 128
SUBLANES = 8
HALO = 16
ROWS = 16
KEY_CHUNK = 256
GS_SLOTS = 3

_BF16 = jnp.bfloat16
_F32 = jnp.float32
_VMEM_LIMIT = 56 * 1024 * 1024


def _cparams(sem):
    return pltpu.CompilerParams(dimension_semantics=sem, vmem_limit_bytes=_VMEM_LIMIT)


def _sigmoid(x):
    return 1.0 / (1.0 + jnp.exp(-x))


def _rms(x, eps):
    return x * lax.rsqrt(jnp.mean(x * x, axis=-1, keepdims=True) + eps)


def _dot(a, b):
    return jnp.dot(a, b, preferred_element_type=_F32)


def _mod_map(layer, tm, seq, nb):
    return lambda i, *_: (layer, jnp.minimum((i * tm) // seq, nb), 0, 0)


def _mod_spec(layer, tm, seq, nb, d):
    return pl.BlockSpec((None, None, N_MOD, d), _mod_map(layer, tm, seq, nb))


def _row_spec(index, width):
    return pl.BlockSpec((None, 1, width), lambda *_: (index, 0, 0))


def _set_gains(gs_ref, mod_ref, j, gpre_ref, gpost_ref, weight):
    rep = lambda row: jnp.broadcast_to(row, (SUBLANES, row.shape[1]))
    if gpre_ref is not None:
        gs_ref[0] = rep(gpre_ref[...] * (1.0 + mod_ref[3 * j + 1:3 * j + 2, :]))
        gs_ref[1] = rep(mod_ref[3 * j:3 * j + 1, :])
    if gpost_ref is not None:
        gs_ref[2] = rep((weight * mod_ref[3 * j + 2:3 * j + 3, :]) * gpost_ref[...])


def _row_stats(src_ref, r_ref, eps, mu_ref=None):
    for i in range(src_ref.shape[0] // ROWS):
        rows = slice(i * ROWS, (i + 1) * ROWS)
        v = src_ref[rows, :]
        if mu_ref is not None:
            mu = jnp.mean(v, axis=-1, keepdims=True)
            mu_ref[rows, :] = jnp.broadcast_to(mu, (ROWS, LANES))
            v = v - mu
        r = lax.rsqrt(jnp.mean(v * v, axis=-1, keepdims=True) + eps)
        r_ref[rows, :] = jnp.broadcast_to(r, (ROWS, LANES))


def _lanes(v, width):
    return jnp.concatenate([v] * (width // LANES), axis=1)


def _sublanes(v):
    return jnp.concatenate([v] * (ROWS // SUBLANES), axis=0)


def _row_loop(n_rows, body):
    def step(i, carry):
        body(pl.ds(pl.multiple_of(i * ROWS, ROWS), ROWS))
        return carry

    lax.fori_loop(0, n_rows // ROWS, step, 0, unroll=2)


def _prenorm_rows(x_ref, h_ref, gs_ref, r_ref):
    d = x_ref.shape[1]
    _row_stats(x_ref, r_ref, NORM_EPS)

    def body(rows):
        h = x_ref[rows, :] * _lanes(r_ref[rows, :], d) * _sublanes(gs_ref[0]) + _sublanes(gs_ref[1])
        h_ref[rows, :] = h.astype(h_ref.dtype)

    _row_loop(x_ref.shape[0], body)


def _postnorm_rows(y_ref, x_ref, o_ref, gs_ref, r_ref):
    d = x_ref.shape[1]
    _row_stats(y_ref, r_ref, NORM_EPS)

    def body(rows):
        o_ref[rows, :] = x_ref[rows, :] + y_ref[rows, :] * _lanes(r_ref[rows, :], d) * _sublanes(gs_ref[2])

    _row_loop(x_ref.shape[0], body)


def _ada_kernel(c_ref, w_ref, b_ref, o_ref):
    c = c_ref[...]
    s = (c * _sigmoid(c)).astype(_BF16)
    o_ref[...] = _dot(s, w_ref[...].astype(_BF16)) + b_ref[...]


def _ada_call(cs, ada_w, ada_b, tn):
    depth, d, n = ada_w.shape
    rows = cs.shape[0]
    return pl.pallas_call(
        _ada_kernel,
        grid=(depth, n // tn),
        in_specs=[
            pl.BlockSpec((rows, d), lambda l, j: (0, 0)),
            pl.BlockSpec((None, d, tn), lambda l, j: (l, 0, j)),
            pl.BlockSpec((None, 1, tn), lambda l, j: (l, 0, j)),
        ],
        out_specs=pl.BlockSpec((None, rows, tn), lambda l, j: (l, 0, j)),
        out_shape=jax.ShapeDtypeStruct((depth, rows, n), _F32),
        compiler_params=_cparams(("parallel", "parallel")),
        name="ada_mod",
    )(cs, ada_w, ada_b.reshape(depth, 1, n))


def _ffn_kernel(*refs, j, nk, n_first):
    sources, refs = refs[:1 if n_first is None else 2], refs[1 if n_first is None else 2:]
    mod_ref, gpre_ref, gpost_ref, wa_ref, wu_ref, wo_ref, o_ref, h_ref, acc_ref, gs_ref, r_ref = refs
    i, k = pl.program_id(0), pl.program_id(1)

    def per_source(fn):
        if n_first is None:
            fn(sources[0])
        else:
            pl.when(i < n_first)(lambda: fn(sources[0]))
            pl.when(i >= n_first)(lambda: fn(sources[1]))

    @pl.when(k == 0)
    def _():
        _set_gains(gs_ref, mod_ref, j, gpre_ref, gpost_ref, 0.5)
        per_source(lambda x_ref: _prenorm_rows(x_ref, h_ref, gs_ref, r_ref))
        acc_ref[...] = jnp.zeros_like(acc_ref)

    h = h_ref[...]
    a = _dot(h, wa_ref[...])
    u = _dot(h, wu_ref[...])
    act = (a * _sigmoid(a) * u).astype(_BF16)
    acc_ref[...] += _dot(act, wo_ref[...])

    @pl.when(k == nk - 1)
    def _():
        per_source(lambda x_ref: _postnorm_rows(acc_ref, x_ref, o_ref, gs_ref, r_ref))


def _ffn_call(xs, mods, norm_pre, norm_post, w_in, w_out, *, layer, j, n_tok, seq, nb, tm, fc, tail=None):
    d = xs.shape[1]
    f = w_out.shape[1]
    nk = f // fc
    n_first = None if tail is None else xs.shape[0] // tm
    if tail is None:
        src_specs, srcs = [pl.BlockSpec((tm, d), lambda i, k: (i, 0))], (xs,)
    else:
        src_specs = [pl.BlockSpec((tm, d), lambda i, k: (jnp.minimum(i, n_first - 1), 0)),
                     pl.BlockSpec((tm, d), lambda i, k: (jnp.maximum(i - n_first, 0), 0),
                                  pipeline_mode=pl.Buffered(1))]
        srcs = (xs, tail)
    return pl.pallas_call(
        functools.partial(_ffn_kernel, j=j, nk=nk, n_first=n_first),
        grid=(n_tok // tm, nk),
        in_specs=src_specs + [
            _mod_spec(layer, tm, seq, nb, d),
            _row_spec(layer * N_SUB + j, d),
            _row_spec(layer * N_SUB + j, d),
            pl.BlockSpec((None, d, fc), lambda i, k: (layer, 0, k)),
            pl.BlockSpec((None, d, fc), lambda i, k: (layer, 0, nk + k)),
            pl.BlockSpec((None, fc, d), lambda i, k: (layer, k, 0)),
        ],
        out_specs=pl.BlockSpec((tm, d), lambda i, k: (i, 0)),
        out_shape=jax.ShapeDtypeStruct((n_tok, d), _F32),
        scratch_shapes=[pltpu.VMEM((tm, d), _BF16), pltpu.VMEM((tm, d), _F32), pltpu.VMEM((GS_SLOTS, SUBLANES, d), _F32),
                        pltpu.VMEM((tm, LANES), _F32)],
        compiler_params=_cparams(("parallel", "arbitrary")),
        name=f"ffn_half{j}",
    )(*srcs, mods, norm_pre, norm_post, w_in, w_in, w_out)


def _pw1_kernel(x_ref, mod_ref, gpre_ref, wa_ref, wg_ref, ba_ref, bg_ref, o_ref, h_ref, gs_ref, r_ref):
    @pl.when(pl.program_id(1) == 0)
    def _():
        _set_gains(gs_ref, mod_ref, 1, gpre_ref, None, 1.0)
        _prenorm_rows(x_ref, h_ref, gs_ref, r_ref)

    h = h_ref[...]
    a = _dot(h, wa_ref[...]) + ba_ref[...]
    g = _dot(h, wg_ref[...]) + bg_ref[...]
    o_ref[...] = (a * _sigmoid(g)).astype(o_ref.dtype)


def _pw1_call(xs, mods, norm_pre, w_pw1, b_pw1, *, layer, li, seq, nb, tm, nc):
    n_tok, d = xs.shape
    nj = d // nc
    return pl.pallas_call(
        _pw1_kernel,
        grid=(n_tok // tm, nj),
        in_specs=[
            pl.BlockSpec((tm, d), lambda i, k: (i, 0)),
            _mod_spec(layer, tm, seq, nb, d),
            _row_spec(layer * N_SUB + 1, d),
            pl.BlockSpec((None, d, nc), lambda i, k: (li, 0, k)),
            pl.BlockSpec((None, d, nc), lambda i, k: (li, 0, nj + k)),
            pl.BlockSpec((None, 1, nc), lambda i, k: (li, 0, k)),
            pl.BlockSpec((None, 1, nc), lambda i, k: (li, 0, nj + k)),
        ],
        out_specs=pl.BlockSpec((tm, nc), lambda i, k: (i, k)),
        out_shape=jax.ShapeDtypeStruct((n_tok, d), _BF16),
        scratch_shapes=[pltpu.VMEM((tm, d), _BF16), pltpu.VMEM((GS_SLOTS, SUBLANES, d), _F32), pltpu.VMEM((tm, LANES), _F32)],
        compiler_params=_cparams(("parallel", "arbitrary")),
        name="conv_pw1_glu",
    )(xs, mods, norm_pre, w_pw1, w_pw1, b_pw1, b_pw1)


def _dwconv_kernel(u_ref, up_ref, un_ref, x_ref, mod_ref, wdw_ref, bdw_ref, lng_ref, lnb_ref, w2_ref, b2_ref,
                   gpost_ref, o_ref, ext_ref, cv_ref, z_ref, gs_ref, r_ref, mu_ref, sh_ref,
                   *, tm, seq, ctx, n_lat, width):
    d = x_ref.shape[1]
    start = pl.program_id(0) * tm
    is_lat = start < n_lat
    pos = jnp.where(is_lat, start % seq, (start - n_lat) % ctx)
    slen = jnp.where(is_lat, seq, ctx)
    keep_prev = (pos != 0).astype(_F32)
    keep_next = (pos + tm != slen).astype(_F32)
    ext_ref[0:HALO, :] = up_ref[...].astype(_F32) * keep_prev
    ext_ref[HALO:HALO + tm, :] = u_ref[...].astype(_F32)
    ext_ref[HALO + tm:2 * HALO + tm, :] = un_ref[...].astype(_F32) * keep_next
    _set_gains(gs_ref, mod_ref, 1, None, gpost_ref, 1.0)

    off = HALO - width // 2
    half = tm // 2
    a_max = (off + width - 1) // SUBLANES * SUBLANES

    def col_block(c, carry):
        cs = pl.ds(pl.multiple_of(c * LANES, LANES), LANES)
        for r0 in (0, half):
            for r in range(1, SUBLANES):
                sh_ref[r, 0:half + a_max, :] = ext_ref[r0 + r:r0 + r + half + a_max, cs]
            acc = jnp.zeros((half, LANES), _F32)
            for r in range(SUBLANES):
                taps = [k for k in range(width) if (off + k) % SUBLANES == r]
                if not taps:
                    continue
                rows_r = ext_ref[r0:r0 + half + a_max, cs] if r == 0 else sh_ref[r, 0:half + a_max, :]
                for k in taps:
                    a = (off + k) // SUBLANES * SUBLANES
                    acc = acc + wdw_ref[k:k + 1, cs] * rows_r[a:a + half, :]
            cv_ref[r0:r0 + half, cs] = acc + bdw_ref[:, cs]
        return carry

    lax.fori_loop(0, d // LANES, col_block, 0)

    _row_stats(cv_ref, r_ref, LN_EPS, mu_ref)

    def ln_rows(rows):
        z = (cv_ref[rows, :] - _lanes(mu_ref[rows, :], d)) * _lanes(r_ref[rows, :], d) * lng_ref[...] + lnb_ref[...]
        z_ref[rows, :] = (z * _sigmoid(z)).astype(z_ref.dtype)

    _row_loop(tm, ln_rows)
    cv_ref[...] = _dot(z_ref[...], w2_ref[...]) + b2_ref[...]
    _postnorm_rows(cv_ref, x_ref, o_ref, gs_ref, r_ref)


def _dwconv_call(u, xs, mods, norm_post, w_dw, b_dw, ln_g, ln_b, w_pw2, b_pw2, *, layer, li, seq, ctx, nb, tm):
    n_tok, d = xs.shape
    width = w_dw.shape[1]
    assert width // 2 <= HALO and tm % (2 * HALO) == 0 and seq % tm == 0 and ctx % tm == 0
    n_lat = nb * seq
    r = tm // HALO
    last_halo = n_tok // HALO - 1
    wrows = -(-width // SUBLANES) * SUBLANES
    wpad = jnp.zeros((w_dw.shape[0], wrows, d), _F32).at[:, :width].set(w_dw)
    return pl.pallas_call(
        functools.partial(_dwconv_kernel, tm=tm, seq=seq, ctx=ctx, n_lat=n_lat, width=width),
        grid=(n_tok // tm,),
        in_specs=[
            pl.BlockSpec((tm, d), lambda i: (i, 0)),
            pl.BlockSpec((HALO, d), lambda i: (jnp.maximum(i * r - 1, 0), 0)),
            pl.BlockSpec((HALO, d), lambda i: (jnp.minimum((i + 1) * r, last_halo), 0)),
            pl.BlockSpec((tm, d), lambda i: (i, 0)),
            _mod_spec(layer, tm, seq, nb, d),
            pl.BlockSpec((None, wrows, d), lambda i: (li, 0, 0)),
            _row_spec(li, d),
            _row_spec(li, d),
            _row_spec(li, d),
            pl.BlockSpec((None, d, d), lambda i: (li, 0, 0)),
            _row_spec(li, d),
            _row_spec(layer * N_SUB + 1, d),
        ],
        out_specs=pl.BlockSpec((tm, d), lambda i: (i, 0)),
        out_shape=jax.ShapeDtypeStruct((n_tok, d), _F32),
        scratch_shapes=[pltpu.VMEM((tm + 2 * HALO, d), _F32), pltpu.VMEM((tm, d), _F32),
                        pltpu.VMEM((tm, d), _BF16), pltpu.VMEM((GS_SLOTS, SUBLANES, d), _F32),
                        pltpu.VMEM((tm, LANES), _F32), pltpu.VMEM((tm, LANES), _F32),
                        pltpu.VMEM((SUBLANES, tm // 2 + 2 * HALO, LANES), _F32)],
        compiler_params=_cparams(("parallel",)),
        name="conv_dw_pw2",
    )(u, u, u, xs, mods, wpad, b_dw, ln_g, ln_b, w_pw2, b_pw2, norm_post)


def _rope_tables(seq, tm):
    t = np.arange(seq, dtype=np.int32)
    inv_freq = (np.float32(ROPE_THETA) ** (-np.arange(0, AXIS_DIM, 2, dtype=np.float32) / np.float32(AXIS_DIM)))
    parts_c, parts_s = [], []
    for pos in (t // GRID_W, t % GRID_W):
        ang = pos.astype(np.float32)[:, None] * inv_freq.astype(np.float32)
        parts_c.append(np.cos(ang).astype(np.float32))
        parts_s.append(np.sin(ang).astype(np.float32))
    cos = np.concatenate(parts_c * 4, axis=1)
    sin = np.concatenate([-s for s in parts_s] * 2 + parts_s * 2, axis=1)
    cos = np.concatenate([cos, np.ones((tm, HEAD_LANES), np.float32)], axis=0)
    sin = np.concatenate([sin, np.zeros((tm, HEAD_LANES), np.float32)], axis=0)
    return jnp.asarray(cos), jnp.asarray(sin)


def _rope_layout(w_qkv):
    layers, d, _ = w_qkv.shape
    half = AXIS_DIM // 2
    qk = w_qkv[:, :, :2 * d].reshape(layers, d, -1, HEAD_LANES // AXIS_DIM, 2, half)
    qk = qk.transpose(0, 1, 2, 4, 3, 5).reshape(layers, d, 2 * d)
    return jnp.concatenate([qk, w_qkv[:, :, 2 * d:]], axis=-1)


def _qkv_kernel(x_ref, mod_ref, gpre_ref, w_ref, cos_ref, sin_ref, o_ref, h_ref, gs_ref, r_ref, *, n_q, scale):
    j = pl.program_id(1)

    @pl.when(j == 0)
    def _():
        _set_gains(gs_ref, mod_ref, 1, gpre_ref, None, 1.0)
        _prenorm_rows(x_ref, h_ref, gs_ref, r_ref)

    y = _dot(h_ref[...], w_ref[...])
    cos = cos_ref[...]
    sin = sin_ref[...]
    sc = jnp.where(j < n_q, scale, 1.0).astype(_F32)
    for hh in range(y.shape[1] // HEAD_LANES):
        ys = y[:, hh * HEAD_LANES:(hh + 1) * HEAD_LANES]
        partner = pltpu.roll(ys, HEAD_LANES // 2, axis=1)
        r = (ys * cos + partner * sin) * sc
        o_ref[:, hh * HEAD_LANES:(hh + 1) * HEAD_LANES] = r.astype(o_ref.dtype)


def _qkv_call(xs, mods, norm_pre, w_qkv, *, layer, li, seq, nb, tm, nc):
    n_tok, d = xs.shape
    n_lat_tiles = nb * seq // tm
    seq_tiles = seq // tm
    cos, sin = _rope_tables(seq, tm)
    n_rope = 2 * d // nc
    tab_map = lambda i, j: (jnp.where((i < n_lat_tiles) & (j < n_rope), i % seq_tiles, seq_tiles), 0)
    return pl.pallas_call(
        functools.partial(_qkv_kernel, n_q=d // nc, scale=HEAD_DIM ** -0.5 * math.log2(math.e)),
        grid=(n_tok // tm, 3 * d // nc),
        in_specs=[
            pl.BlockSpec((tm, d), lambda i, j: (i, 0)),
            _mod_spec(layer, tm, seq, nb, d),
            _row_spec(layer * N_SUB + 1, d),
            pl.BlockSpec((None, d, nc), lambda i, j: (li, 0, j)),
            pl.BlockSpec((tm, HEAD_LANES), tab_map),
            pl.BlockSpec((tm, HEAD_LANES), tab_map),
        ],
        out_specs=pl.BlockSpec((tm, nc), lambda i, j: (i, j)),
        out_shape=jax.ShapeDtypeStruct((n_tok, 3 * d), _BF16),
        scratch_shapes=[pltpu.VMEM((tm, d), _BF16), pltpu.VMEM((GS_SLOTS, SUBLANES, d), _F32), pltpu.VMEM((tm, LANES), _F32)],
        compiler_params=_cparams(("parallel", "arbitrary")),
        name="attn_qkv_rope",
    )(xs, mods, norm_pre, w_qkv, cos, sin)


def _attn_kernel(q_ref, kl_ref, kc_ref, vl_ref, vc_ref, lam_ref, g_ref, o_ref, k_ref, vt_ref,
                 s0_ref, s1_ref, m0_ref, m1_ref, *, lam_init, seq, n_tiles):
    step = pl.program_id(2)

    @pl.when(step == 0)
    def _():
        k_ref[0:seq, :] = kl_ref[...]
        k_ref[seq:, :] = kc_ref[...]
        vt_ref[0:HEAD_LANES, 0:seq] = vl_ref[...].astype(_F32).T.astype(_BF16)
        vt_ref[0:HEAD_LANES, seq:] = vc_ref[...].astype(_F32).T.astype(_BF16)

    lp = lam_ref[...]
    lam = (jnp.exp(jnp.sum(lp[0:1] * lp[1:2], axis=-1, keepdims=True))
           - jnp.exp(jnp.sum(lp[2:3] * lp[3:4], axis=-1, keepdims=True)) + lam_init)
    tq = q_ref.shape[0]
    keys = k_ref.shape[0]

    def pipeline_step(new, old):
        if new is not None:
            s_new, m_new = new
            q = q_ref[...]
            comp0 = lax.broadcasted_iota(jnp.int32, q.shape, 1) % HEAD_DIM < HEAD_DIM // 2
            zero = jnp.zeros_like(q)
            q2 = jnp.concatenate([jnp.where(comp0, q, zero), jnp.where(comp0, zero, q)], axis=0)
            q2t = q2.astype(_F32).T.astype(_BF16)
        if old is not None:
            s_old, m_old = old
            m_prev = m_old[...]
        acc = den = mx = None
        for c in range(keys // KEY_CHUNK):
            rows = slice(c * KEY_CHUNK, (c + 1) * KEY_CHUNK)
            if old is not None:
                e = jnp.exp2(s_old[rows, :] - m_prev)
                ls = jnp.sum(e, axis=0, keepdims=True)
                den = ls if den is None else den + ls
                part = _dot(vt_ref[:, rows], e.astype(_BF16))
                acc = part if acc is None else acc + part
            if new is not None:
                st = _dot(k_ref[rows, :], q2t)
                s_new[rows, :] = st
                cm = jnp.max(st, axis=0, keepdims=True)
                mx = cm if mx is None else jnp.maximum(mx, cm)
        if new is not None:
            m_new[...] = mx
        if old is not None:
            w = 1.0 / den
            o = (acc[:, 0:tq] * w[:, 0:tq] - acc[:, tq:] * (lam * w[:, tq:])).T
            o = _rms(o, SUBLN_EPS) * g_ref[...] * (1.0 - lam_init)
            o_ref[...] = o.astype(o_ref.dtype)

    bufs = ((s0_ref, m0_ref), (s1_ref, m1_ref))
    middle = jnp.logical_and(step > 0, step < n_tiles)
    pl.when(step == 0)(lambda: pipeline_step(bufs[0], None))
    pl.when(jnp.logical_and(middle, step % 2 == 0))(lambda: pipeline_step(bufs[0], bufs[1]))
    pl.when(jnp.logical_and(middle, step % 2 == 1))(lambda: pipeline_step(bufs[1], bufs[0]))
    pl.when(step == n_tiles)(lambda: pipeline_step(None, bufs[(n_tiles - 1) % 2]))


def _attn_call(qkv, lam_params, subln_g, *, li, lam_init, seq, ctx, nb, tq):
    d = qkv.shape[1] // 3
    nh = d // HEAD_LANES
    n_lat = nb * seq
    qt = seq // tq
    keys = seq + ctx
    return pl.pallas_call(
        functools.partial(_attn_kernel, lam_init=lam_init, seq=seq, n_tiles=qt),
        grid=(nb, nh, qt + 1),
        in_specs=[
            pl.BlockSpec((tq, HEAD_LANES), lambda b, h, i: (b * qt + jnp.minimum(i, qt - 1), h)),
            pl.BlockSpec((seq, HEAD_LANES), lambda b, h, i: (b, nh + h)),
            pl.BlockSpec((ctx, HEAD_LANES), lambda b, h, i: (n_lat // ctx + b, nh + h)),
            pl.BlockSpec((seq, HEAD_LANES), lambda b, h, i: (b, 2 * nh + h)),
            pl.BlockSpec((ctx, HEAD_LANES), lambda b, h, i: (n_lat // ctx + b, 2 * nh + h)),
            pl.BlockSpec((None, 4, HEAD_DIM), lambda b, h, i: (li, 0, 0)),
            _row_spec(li, HEAD_LANES),
        ],
        out_specs=pl.BlockSpec((tq, HEAD_LANES), lambda b, h, i: (b * qt + jnp.maximum(i - 1, 0), h)),
        out_shape=jax.ShapeDtypeStruct((n_lat, d), _BF16),
        scratch_shapes=[pltpu.VMEM((keys, HEAD_LANES), _BF16),
                        pltpu.VMEM((HEAD_LANES, keys), _BF16),
                        pltpu.VMEM((keys, 2 * tq), _F32), pltpu.VMEM((keys, 2 * tq), _F32),
                        pltpu.VMEM((1, 2 * tq), _F32), pltpu.VMEM((1, 2 * tq), _F32)],
        compiler_params=_cparams(("parallel", "parallel", "arbitrary")),
        name="diff_attn",
    )(qkv, qkv, qkv, qkv, qkv, lam_params, subln_g)


def _attn_out_kernel(o_ref, x_ref, mod_ref, w_ref, gpost_ref, out_ref, y_ref, gs_ref, r_ref):
    _set_gains(gs_ref, mod_ref, 1, None, gpost_ref, 1.0)
    y_ref[...] = _dot(o_ref[...], w_ref[...])
    _postnorm_rows(y_ref, x_ref, out_ref, gs_ref, r_ref)


def _attn_out_call(o, xs, mods, norm_post, w_o, *, layer, li, seq, nb, tm):
    n_lat, d = o.shape
    return pl.pallas_call(
        _attn_out_kernel,
        grid=(n_lat // tm,),
        in_specs=[
            pl.BlockSpec((tm, d), lambda i: (i, 0)),
            pl.BlockSpec((tm, d), lambda i: (i, 0)),
            _mod_spec(layer, tm, seq, nb, d),
            pl.BlockSpec((None, d, d), lambda i: (li, 0, 0)),
            _row_spec(layer * N_SUB + 1, d),
        ],
        out_specs=pl.BlockSpec((tm, d), lambda i: (i, 0)),
        out_shape=jax.ShapeDtypeStruct((n_lat, d), _F32),
        scratch_shapes=[pltpu.VMEM((tm, d), _F32), pltpu.VMEM((GS_SLOTS, SUBLANES, d), _F32), pltpu.VMEM((tm, LANES), _F32)],
        compiler_params=_cparams(("parallel",)),
        name="attn_out_proj",
    )(o, xs, mods, w_o, norm_post)


def _tiles(n_lat_seq, n_ctx_total, d, f):
    tm = min(512, n_lat_seq, n_ctx_total)
    return dict(tm=tm, fc=min(512, f), nc=min(2048, d), nc_qkv=min(2048, d), tconv=min(256, n_lat_seq),
                tq=min(512, n_lat_seq))


def kernel(x, c, ctx, c_ctx, ada_w, ada_b, norm_pre, norm_post, ffn1_w_in, ffn1_w_out, ffn2_w_in, ffn2_w_out,
           conv_w_pw1, conv_b_pw1, conv_w_dw, conv_b_dw, conv_ln_g, conv_ln_b, conv_w_pw2, conv_b_pw2,
           attn_w_qkv, attn_lambda_q1, attn_lambda_k1, attn_lambda_q2, attn_lambda_k2, attn_subln_g, attn_w_o):
    nb, seq, d = x.shape
    m = ctx.shape[1]
    depth = ada_w.shape[0]
    f = ffn1_w_out.shape[1]
    n_lat, n_ctx = nb * seq, nb * m
    t = _tiles(seq, n_ctx, d, f)
    tm, fc, nc = t["tm"], t["fc"], t["nc"]
    assert seq % tm == 0 and n_ctx % tm == 0 and f % fc == 0 and d % nc == 0 and nc % HEAD_LANES == 0
    assert seq % GRID_W == 0 and d % HEAD_LANES == 0

    bf = lambda w: w.astype(_BF16)
    rows3 = lambda v: v.reshape(-1, 1, v.shape[-1])

    rows = -(-(nb + 1) // SUBLANES) * SUBLANES
    cs = jnp.zeros((rows, d), _F32).at[:nb].set(c).at[nb].set(c_ctx)
    mods = _ada_call(cs, ada_w, ada_b, min(1024, d)).reshape(depth, rows, N_MOD, d)

    g_pre, g_post = rows3(norm_pre), rows3(norm_post)
    ffn_w = ((bf(ffn1_w_in), bf(ffn1_w_out)), (bf(ffn2_w_in), bf(ffn2_w_out)))
    w_pw1, w_pw2, w_qkv, w_o = bf(conv_w_pw1), bf(conv_w_pw2), _rope_layout(bf(attn_w_qkv)), bf(attn_w_o)
    lam_params = jnp.stack([attn_lambda_q1, attn_lambda_k1, attn_lambda_q2, attn_lambda_k2], axis=1)

    xs, tail = x.reshape(n_lat, d), ctx.reshape(n_ctx, d)
    n_all = n_lat + n_ctx

    for i in range(depth):
        last = i == depth - 1
        is_conv = i % 2 == 0
        li = i // 2
        ctx_live = not (last and is_conv)
        common = dict(layer=i, seq=seq, nb=nb)

        xs = _ffn_call(xs, mods, g_pre, g_post, *ffn_w[0], j=0, n_tok=n_all if ctx_live else n_lat,
                       tm=tm, fc=fc, tail=tail if ctx_live else None, **common)
        tail = None

        if is_conv:
            if last and xs.shape[0] != n_lat:
                xs = xs[:n_lat]
            u = _pw1_call(xs, mods, g_pre, w_pw1, rows3(conv_b_pw1), li=li, tm=tm, nc=nc, **common)
            xs = _dwconv_call(u, xs, mods, g_post, conv_w_dw, rows3(conv_b_dw), rows3(conv_ln_g), rows3(conv_ln_b),
                              w_pw2, rows3(conv_b_pw2), li=li, ctx=m, tm=t["tconv"], **common)
        else:
            assert last, "attention layers before the last one are not supported"
            lam_init = 0.8 - 0.6 * math.exp(-0.3 * i)
            qkv = _qkv_call(xs, mods, g_pre, w_qkv, li=li, tm=tm, nc=t["nc_qkv"], **common)
            o = _attn_call(qkv, lam_params, rows3(attn_subln_g), li=li, lam_init=lam_init, seq=seq, ctx=m, nb=nb,
                           tq=t["tq"])
            xs = _attn_out_call(o, xs, mods, g_post, w_o, li=li, tm=tm, **common)

        xs = _ffn_call(xs, mods, g_pre, g_post, *ffn_w[1], j=2, n_tok=n_lat if last else n_all,
                       tm=tm, fc=fc, **common)

    return xs[:n_lat].reshape(nb, seq, d)
```

```python
import functools
import math

import jax
import jax.numpy as jnp
import numpy as np
from jax import lax
from jax.experimental import pallas as pl
from jax.experimental.pallas import tpu as pltpu

NORM_EPS = 1e-6
SUBLN_EPS = 1e-5
LN_EPS = 1e-5
GRID_W = 64
HEAD_DIM = 64
HEAD_LANES = 2 * HEAD_DIM
AXIS_DIM = HEAD_DIM // 2
ROPE_THETA = 10000.0
N_SUB = 3
N_MOD = 3 * N_SUB
LANES = 128
SUBLANES = 8
HALO = 16
ROWS = 16
KEY_CHUNK = 256
GS_SLOTS = 3

_BF16 = jnp.bfloat16
_F32 = jnp.float32
_VMEM_LIMIT = 56 * 1024 * 1024


def _cparams(sem):
    return pltpu.CompilerParams(dimension_semantics=sem, vmem_limit_bytes=_VMEM_LIMIT)


def _sigmoid(x):
    return 1.0 / (1.0 + jnp.exp(-x))


def _rms(x, eps):
    return x * lax.rsqrt(jnp.mean(x * x, axis=-1, keepdims=True) + eps)


def _dot(a, b):
    return jnp.dot(a, b, preferred_element_type=_F32)


def _mod_map(layer, tm, seq, nb):
    return lambda i, *_: (layer, jnp.minimum((i * tm) // seq, nb), 0, 0)


def _mod_spec(layer, tm, seq, nb, d):
    return pl.BlockSpec((None, None, N_MOD, d), _mod_map(layer, tm, seq, nb))


def _row_spec(index, width):
    return pl.BlockSpec((None, 1, width), lambda *_: (index, 0, 0))


def _set_gains(gs_ref, mod_ref, j, gpre_ref, gpost_ref, weight):
    rep = lambda row: jnp.broadcast_to(row, (SUBLANES, row.shape[1]))
    if gpre_ref is not None:
        gs_ref[0] = rep(gpre_ref[...] * (1.0 + mod_ref[3 * j + 1:3 * j + 2, :]))
        gs_ref[1] = rep(mod_ref[3 * j:3 * j + 1, :])
    if gpost_ref is not None:
        gs_ref[2] = rep((weight * mod_ref[3 * j + 2:3 * j + 3, :]) * gpost_ref[...])


def _row_stats(src_ref, r_ref, eps, mu_ref=None):
    for i in range(src_ref.shape[0] // ROWS):
        rows = slice(i * ROWS, (i + 1) * ROWS)
        v = src_ref[rows, :]
        if mu_ref is not None:
            mu = jnp.mean(v, axis=-1, keepdims=True)
            mu_ref[rows, :] = jnp.broadcast_to(mu, (ROWS, LANES))
            v = v - mu
        r = lax.rsqrt(jnp.mean(v * v, axis=-1, keepdims=True) + eps)
        r_ref[rows, :] = jnp.broadcast_to(r, (ROWS, LANES))


def _lanes(v, width):
    return jnp.concatenate([v] * (width // LANES), axis=1)


def _sublanes(v):
    return jnp.concatenate([v] * (ROWS // SUBLANES), axis=0)


def _row_loop(n_rows, body):
    def step(i, carry):
        body(pl.ds(pl.multiple_of(i * ROWS, ROWS), ROWS))
        return carry

    lax.fori_loop(0, n_rows // ROWS, step, 0, unroll=2)


def _prenorm_rows(x_ref, h_ref, gs_ref, r_ref):
    d = x_ref.shape[1]
    _row_stats(x_ref, r_ref, NORM_EPS)

    def body(rows):
        h = x_ref[rows, :] * _lanes(r_ref[rows, :], d) * _sublanes(gs_ref[0]) + _sublanes(gs_ref[1])
        h_ref[rows, :] = h.astype(h_ref.dtype)

    _row_loop(x_ref.shape[0], body)


def _postnorm_rows(y_ref, x_ref, o_ref, gs_ref, r_ref):
    d = x_ref.shape[1]
    _row_stats(y_ref, r_ref, NORM_EPS)

    def body(rows):
        o_ref[rows, :] = x_ref[rows, :] + y_ref[rows, :] * _lanes(r_ref[rows, :], d) * _sublanes(gs_ref[2])

    _row_loop(x_ref.shape[0], body)


def _ada_kernel(c_ref, w_ref, b_ref, o_ref):
    c = c_ref[...]
    s = (c * _sigmoid(c)).astype(_BF16)
    o_ref[...] = _dot(s, w_ref[...].astype(_BF16)) + b_ref[...]


def _ada_call(cs, ada_w, ada_b, tn):
    depth, d, n = ada_w.shape
    rows = cs.shape[0]
    return pl.pallas_call(
        _ada_kernel,
        grid=(depth, n // tn),
        in_specs=[
            pl.BlockSpec((rows, d), lambda l, j: (0, 0)),
            pl.BlockSpec((None, d, tn), lambda l, j: (l, 0, j)),
            pl.BlockSpec((None, 1, tn), lambda l, j: (l, 0, j)),
        ],
        out_specs=pl.BlockSpec((None, rows, tn), lambda l, j: (l, 0, j)),
        out_shape=jax.ShapeDtypeStruct((depth, rows, n), _F32),
        compiler_params=_cparams(("parallel", "parallel")),
        name="ada_mod",
    )(cs, ada_w, ada_b.reshape(depth, 1, n))


def _ffn_kernel(*refs, j, nk, n_first):
    sources, refs = refs[:1 if n_first is None else 2], refs[1 if n_first is None else 2:]
    mod_ref, gpre_ref, gpost_ref, wa_ref, wu_ref, wo_ref, o_ref, h_ref, acc_ref, gs_ref, r_ref = refs
    i, k = pl.program_id(0), pl.program_id(1)

    def per_source(fn):
        if n_first is None:
            fn(sources[0])
        else:
            pl.when(i < n_first)(lambda: fn(sources[0]))
            pl.when(i >= n_first)(lambda: fn(sources[1]))

    @pl.when(k == 0)
    def _():
        _set_gains(gs_ref, mod_ref, j, gpre_ref, gpost_ref, 0.5)
        per_source(lambda x_ref: _prenorm_rows(x_ref, h_ref, gs_ref, r_ref))
        acc_ref[...] = jnp.zeros_like(acc_ref)

    h = h_ref[...]
    a = _dot(h, wa_ref[...])
    u = _dot(h, wu_ref[...])
    act = (a * _sigmoid(a) * u).astype(_BF16)
    acc_ref[...] += _dot(act, wo_ref[...])

    @pl.when(k == nk - 1)
    def _():
        per_source(lambda x_ref: _postnorm_rows(acc_ref, x_ref, o_ref, gs_ref, r_ref))


def _ffn_call(xs, mods, norm_pre, norm_post, w_in, w_out, *, layer, j, n_tok, seq, nb, tm, fc, tail=None):
    d = xs.shape[1]
    f = w_out.shape[1]
    nk = f // fc
    n_first = None if tail is None else xs.shape[0] // tm
    if tail is None:
        src_specs, srcs = [pl.BlockSpec((tm, d), lambda i, k: (i, 0))], (xs,)
    else:
        src_specs = [pl.BlockSpec((tm, d), lambda i, k: (jnp.minimum(i, n_first - 1), 0)),
                     pl.BlockSpec((tm, d), lambda i, k: (jnp.maximum(i - n_first, 0), 0),
                                  pipeline_mode=pl.Buffered(1))]
        srcs = (xs, tail)
    return pl.pallas_call(
        functools.partial(_ffn_kernel, j=j, nk=nk, n_first=n_first),
        grid=(n_tok // tm, nk),
        in_specs=src_specs + [
            _mod_spec(layer, tm, seq, nb, d),
            _row_spec(layer * N_SUB + j, d),
            _row_spec(layer * N_SUB + j, d),
            pl.BlockSpec((None, d, fc), lambda i, k: (layer, 0, k)),
            pl.BlockSpec((None, d, fc), lambda i, k: (layer, 0, nk + k)),
            pl.BlockSpec((None, fc, d), lambda i, k: (layer, k, 0)),
        ],
        out_specs=pl.BlockSpec((tm, d), lambda i, k: (i, 0)),
        out_shape=jax.ShapeDtypeStruct((n_tok, d), _F32),
        scratch_shapes=[pltpu.VMEM((tm, d), _BF16), pltpu.VMEM((tm, d), _F32), pltpu.VMEM((GS_SLOTS, SUBLANES, d), _F32),
                        pltpu.VMEM((tm, LANES), _F32)],
        compiler_params=_cparams(("parallel", "arbitrary")),
        name=f"ffn_half{j}",
    )(*srcs, mods, norm_pre, norm_post, w_in, w_in, w_out)


def _pw1_kernel(x_ref, mod_ref, gpre_ref, wa_ref, wg_ref, ba_ref, bg_ref, o_ref, h_ref, gs_ref, r_ref):
    @pl.when(pl.program_id(1) == 0)
    def _():
        _set_gains(gs_ref, mod_ref, 1, gpre_ref, None, 1.0)
        _prenorm_rows(x_ref, h_ref, gs_ref, r_ref)

    h = h_ref[...]
    a = _dot(h, wa_ref[...]) + ba_ref[...]
    g = _dot(h, wg_ref[...]) + bg_ref[...]
    o_ref[...] = (a * _sigmoid(g)).astype(o_ref.dtype)


def _pw1_call(xs, mods, norm_pre, w_pw1, b_pw1, *, layer, li, seq, nb, tm, nc):
    n_tok, d = xs.shape
    nj = d // nc
    return pl.pallas_call(
        _pw1_kernel,
        grid=(n_tok // tm, nj),
        in_specs=[
            pl.BlockSpec((tm, d), lambda i, k: (i, 0)),
            _mod_spec(layer, tm, seq, nb, d),
            _row_spec(layer * N_SUB + 1, d),
            pl.BlockSpec((None, d, nc), lambda i, k: (li, 0, k)),
            pl.BlockSpec((None, d, nc), lambda i, k: (li, 0, nj + k)),
            pl.BlockSpec((None, 1, nc), lambda i, k: (li, 0, k)),
            pl.BlockSpec((None, 1, nc), lambda i, k: (li, 0, nj + k)),
        ],
        out_specs=pl.BlockSpec((tm, nc), lambda i, k: (i, k)),
        out_shape=jax.ShapeDtypeStruct((n_tok, d), _BF16),
        scratch_shapes=[pltpu.VMEM((tm, d), _BF16), pltpu.VMEM((GS_SLOTS, SUBLANES, d), _F32), pltpu.VMEM((tm, LANES), _F32)],
        compiler_params=_cparams(("parallel", "arbitrary")),
        name="conv_pw1_glu",
    )(xs, mods, norm_pre, w_pw1, w_pw1, b_pw1, b_pw1)


def _dwconv_kernel(u_ref, up_ref, un_ref, wdw_ref, bdw_ref, lng_ref, lnb_ref, z_ref, ext_ref, cv_ref, r_ref, mu_ref,
                   sh_ref, *, tm, seq, ctx, n_lat, width):
    d = u_ref.shape[1]
    start = pl.program_id(0) * tm
    is_lat = start < n_lat
    pos = jnp.where(is_lat, start % seq, (start - n_lat) % ctx)
    slen = jnp.where(is_lat, seq, ctx)
    keep_prev = (pos != 0).astype(_F32)
    keep_next = (pos + tm != slen).astype(_F32)
    ext_ref[0:HALO, :] = up_ref[...].astype(_F32) * keep_prev
    ext_ref[HALO:HALO + tm, :] = u_ref[...].astype(_F32)
    ext_ref[HALO + tm:2 * HALO + tm, :] = un_ref[...].astype(_F32) * keep_next

    off = HALO - width // 2
    half = tm // 2
    a_max = (off + width - 1) // SUBLANES * SUBLANES

    def col_block(c, carry):
        cs = pl.ds(pl.multiple_of(c * LANES, LANES), LANES)
        for r0 in (0, half):
            for r in range(1, SUBLANES):
                sh_ref[r, 0:half + a_max, :] = ext_ref[r0 + r:r0 + r + half + a_max, cs]
            acc = jnp.zeros((half, LANES), _F32)
            for r in range(SUBLANES):
                taps = [k for k in range(width) if (off + k) % SUBLANES == r]
                if not taps:
                    continue
                rows_r = ext_ref[r0:r0 + half + a_max, cs] if r == 0 else sh_ref[r, 0:half + a_max, :]
                for k in taps:
                    a = (off + k) // SUBLANES * SUBLANES
                    acc = acc + wdw_ref[k:k + 1, cs] * rows_r[a:a + half, :]
            cv_ref[r0:r0 + half, cs] = acc + bdw_ref[:, cs]
        return carry

    lax.fori_loop(0, d // LANES, col_block, 0)

    _row_stats(cv_ref, r_ref, LN_EPS, mu_ref)

    def ln_rows(rows):
        z = (cv_ref[rows, :] - _lanes(mu_ref[rows, :], d)) * _lanes(r_ref[rows, :], d) * lng_ref[...] + lnb_ref[...]
        z_ref[rows, :] = (z * _sigmoid(z)).astype(z_ref.dtype)

    _row_loop(tm, ln_rows)


def _dwconv_call(u, w_dw, b_dw, ln_g, ln_b, *, li, seq, ctx, nb, tm):
    n_tok, d = u.shape
    width = w_dw.shape[1]
    assert width // 2 <= HALO and tm % (2 * HALO) == 0 and seq % tm == 0 and ctx % tm == 0
    n_lat = nb * seq
    r = tm // HALO
    last_halo = n_tok // HALO - 1
    wrows = -(-width // SUBLANES) * SUBLANES
    wpad = jnp.zeros((w_dw.shape[0], wrows, d), _F32).at[:, :width].set(w_dw)
    return pl.pallas_call(
        functools.partial(_dwconv_kernel, tm=tm, seq=seq, ctx=ctx, n_lat=n_lat, width=width),
        grid=(n_tok // tm,),
        in_specs=[
            pl.BlockSpec((tm, d), lambda i: (i, 0)),
            pl.BlockSpec((HALO, d), lambda i: (jnp.maximum(i * r - 1, 0), 0)),
            pl.BlockSpec((HALO, d), lambda i: (jnp.minimum((i + 1) * r, last_halo), 0)),
            pl.BlockSpec((None, wrows, d), lambda i: (li, 0, 0)),
            _row_spec(li, d),
            _row_spec(li, d),
            _row_spec(li, d),
        ],
        out_specs=pl.BlockSpec((tm, d), lambda i: (i, 0)),
        out_shape=jax.ShapeDtypeStruct((n_tok, d), _BF16),
        scratch_shapes=[pltpu.VMEM((tm + 2 * HALO, d), _F32), pltpu.VMEM((tm, d), _F32),
                        pltpu.VMEM((tm, LANES), _F32), pltpu.VMEM((tm, LANES), _F32),
                        pltpu.VMEM((SUBLANES, tm // 2 + 2 * HALO, LANES), _F32)],
        compiler_params=_cparams(("parallel",)),
        name="conv_dw_ln",
    )(u, u, u, wpad, b_dw, ln_g, ln_b)


def _rope_tables(seq, tm):
    t = np.arange(seq, dtype=np.int32)
    inv_freq = (np.float32(ROPE_THETA) ** (-np.arange(0, AXIS_DIM, 2, dtype=np.float32) / np.float32(AXIS_DIM)))
    parts_c, parts_s = [], []
    for pos in (t // GRID_W, t % GRID_W):
        ang = pos.astype(np.float32)[:, None] * inv_freq.astype(np.float32)
        parts_c.append(np.cos(ang).astype(np.float32))
        parts_s.append(np.sin(ang).astype(np.float32))
    cos = np.concatenate(parts_c * 4, axis=1)
    sin = np.concatenate([-s for s in parts_s] * 2 + parts_s * 2, axis=1)
    cos = np.concatenate([cos, np.ones((tm, HEAD_LANES), np.float32)], axis=0)
    sin = np.concatenate([sin, np.zeros((tm, HEAD_LANES), np.float32)], axis=0)
    return jnp.asarray(cos), jnp.asarray(sin)


def _rope_layout(w_qkv):
    layers, d, _ = w_qkv.shape
    half = AXIS_DIM // 2
    qk = w_qkv[:, :, :2 * d].reshape(layers, d, -1, HEAD_LANES // AXIS_DIM, 2, half)
    qk = qk.transpose(0, 1, 2, 4, 3, 5).reshape(layers, d, 2 * d)
    return jnp.concatenate([qk, w_qkv[:, :, 2 * d:]], axis=-1)


def _qkv_kernel(x_ref, mod_ref, gpre_ref, w_ref, cos_ref, sin_ref, o_ref, h_ref, gs_ref, r_ref, *, n_q, scale):
    j = pl.program_id(1)

    @pl.when(j == 0)
    def _():
        _set_gains(gs_ref, mod_ref, 1, gpre_ref, None, 1.0)
        _prenorm_rows(x_ref, h_ref, gs_ref, r_ref)

    y = _dot(h_ref[...], w_ref[...])
    cos = cos_ref[...]
    sin = sin_ref[...]
    sc = jnp.where(j < n_q, scale, 1.0).astype(_F32)
    for hh in range(y.shape[1] // HEAD_LANES):
        ys = y[:, hh * HEAD_LANES:(hh + 1) * HEAD_LANES]
        partner = pltpu.roll(ys, HEAD_LANES // 2, axis=1)
        r = (ys * cos + partner * sin) * sc
        o_ref[:, hh * HEAD_LANES:(hh + 1) * HEAD_LANES] = r.astype(o_ref.dtype)


def _qkv_call(xs, mods, norm_pre, w_qkv, *, layer, li, seq, nb, tm, nc):
    n_tok, d = xs.shape
    n_lat_tiles = nb * seq // tm
    seq_tiles = seq // tm
    cos, sin = _rope_tables(seq, tm)
    n_rope = 2 * d // nc
    tab_map = lambda i, j: (jnp.where((i < n_lat_tiles) & (j < n_rope), i % seq_tiles, seq_tiles), 0)
    return pl.pallas_call(
        functools.partial(_qkv_kernel, n_q=d // nc, scale=HEAD_DIM ** -0.5 * math.log2(math.e)),
        grid=(n_tok // tm, 3 * d // nc),
        in_specs=[
            pl.BlockSpec((tm, d), lambda i, j: (i, 0)),
            _mod_spec(layer, tm, seq, nb, d),
            _row_spec(layer * N_SUB + 1, d),
            pl.BlockSpec((None, d, nc), lambda i, j: (li, 0, j)),
            pl.BlockSpec((tm, HEAD_LANES), tab_map),
            pl.BlockSpec((tm, HEAD_LANES), tab_map),
        ],
        out_specs=pl.BlockSpec((tm, nc), lambda i, j: (i, j)),
        out_shape=jax.ShapeDtypeStruct((n_tok, 3 * d), _BF16),
        scratch_shapes=[pltpu.VMEM((tm, d), _BF16), pltpu.VMEM((GS_SLOTS, SUBLANES, d), _F32), pltpu.VMEM((tm, LANES), _F32)],
        compiler_params=_cparams(("parallel", "arbitrary")),
        name="attn_qkv_rope",
    )(xs, mods, norm_pre, w_qkv, cos, sin)


def _attn_kernel(q_ref, kl_ref, kc_ref, vl_ref, vc_ref, lam_ref, g_ref, o_ref, k_ref, vt_ref,
                 s0_ref, s1_ref, m0_ref, m1_ref, *, lam_init, seq, n_tiles):
    step = pl.program_id(2)

    @pl.when(step == 0)
    def _():
        k_ref[0:seq, :] = kl_ref[...]
        k_ref[seq:, :] = kc_ref[...]
        vt_ref[0:HEAD_LANES, 0:seq] = vl_ref[...].astype(_F32).T.astype(_BF16)
        vt_ref[0:HEAD_LANES, seq:] = vc_ref[...].astype(_F32).T.astype(_BF16)

    lp = lam_ref[...]
    lam = (jnp.exp(jnp.sum(lp[0:1] * lp[1:2], axis=-1, keepdims=True))
           - jnp.exp(jnp.sum(lp[2:3] * lp[3:4], axis=-1, keepdims=True)) + lam_init)
    tq = q_ref.shape[0]
    keys = k_ref.shape[0]

    def pipeline_step(new, old):
        if new is not None:
            s_new, m_new = new
            q = q_ref[...]
            comp0 = lax.broadcasted_iota(jnp.int32, q.shape, 1) % HEAD_DIM < HEAD_DIM // 2
            zero = jnp.zeros_like(q)
            q2 = jnp.concatenate([jnp.where(comp0, q, zero), jnp.where(comp0, zero, q)], axis=0)
            q2t = q2.astype(_F32).T.astype(_BF16)
        if old is not None:
            s_old, m_old = old
            m_prev = m_old[...]
        acc = den = mx = None
        for c in range(keys // KEY_CHUNK):
            rows = slice(c * KEY_CHUNK, (c + 1) * KEY_CHUNK)
            if old is not None:
                e = jnp.exp2(s_old[rows, :] - m_prev)
                ls = jnp.sum(e, axis=0, keepdims=True)
                den = ls if den is None else den + ls
                part = _dot(vt_ref[:, rows], e.astype(_BF16))
                acc = part if acc is None else acc + part
            if new is not None:
                st = _dot(k_ref[rows, :], q2t)
                s_new[rows, :] = st
                cm = jnp.max(st, axis=0, keepdims=True)
                mx = cm if mx is None else jnp.maximum(mx, cm)
        if new is not None:
            m_new[...] = mx
        if old is not None:
            w = 1.0 / den
            o = (acc[:, 0:tq] * w[:, 0:tq] - acc[:, tq:] * (lam * w[:, tq:])).T
            o = _rms(o, SUBLN_EPS) * g_ref[...] * (1.0 - lam_init)
            o_ref[...] = o.astype(o_ref.dtype)

    bufs = ((s0_ref, m0_ref), (s1_ref, m1_ref))
    middle = jnp.logical_and(step > 0, step < n_tiles)
    pl.when(step == 0)(lambda: pipeline_step(bufs[0], None))
    pl.when(jnp.logical_and(middle, step % 2 == 0))(lambda: pipeline_step(bufs[0], bufs[1]))
    pl.when(jnp.logical_and(middle, step % 2 == 1))(lambda: pipeline_step(bufs[1], bufs[0]))
    pl.when(step == n_tiles)(lambda: pipeline_step(None, bufs[(n_tiles - 1) % 2]))


def _attn_call(qkv, lam_params, subln_g, *, li, lam_init, seq, ctx, nb, tq):
    d = qkv.shape[1] // 3
    nh = d // HEAD_LANES
    n_lat = nb * seq
    qt = seq // tq
    keys = seq + ctx
    return pl.pallas_call(
        functools.partial(_attn_kernel, lam_init=lam_init, seq=seq, n_tiles=qt),
        grid=(nb, nh, qt + 1),
        in_specs=[
            pl.BlockSpec((tq, HEAD_LANES), lambda b, h, i: (b * qt + jnp.minimum(i, qt - 1), h)),
            pl.BlockSpec((seq, HEAD_LANES), lambda b, h, i: (b, nh + h)),
            pl.BlockSpec((ctx, HEAD_LANES), lambda b, h, i: (n_lat // ctx + b, nh + h)),
            pl.BlockSpec((seq, HEAD_LANES), lambda b, h, i: (b, 2 * nh + h)),
            pl.BlockSpec((ctx, HEAD_LANES), lambda b, h, i: (n_lat // ctx + b, 2 * nh + h)),
            pl.BlockSpec((None, 4, HEAD_DIM), lambda b, h, i: (li, 0, 0)),
            _row_spec(li, HEAD_LANES),
        ],
        out_specs=pl.BlockSpec((tq, HEAD_LANES), lambda b, h, i: (b * qt + jnp.maximum(i - 1, 0), h)),
        out_shape=jax.ShapeDtypeStruct((n_lat, d), _BF16),
        scratch_shapes=[pltpu.VMEM((keys, HEAD_LANES), _BF16),
                        pltpu.VMEM((HEAD_LANES, keys), _BF16),
                        pltpu.VMEM((keys, 2 * tq), _F32), pltpu.VMEM((keys, 2 * tq), _F32),
                        pltpu.VMEM((1, 2 * tq), _F32), pltpu.VMEM((1, 2 * tq), _F32)],
        compiler_params=_cparams(("parallel", "parallel", "arbitrary")),
        name="diff_attn",
    )(qkv, qkv, qkv, qkv, qkv, lam_params, subln_g)


def _mixer_out_kernel(*refs, has_bias):
    o_ref, x_ref, mod_ref, w_ref = refs[:4]
    b_ref = refs[4] if has_bias else None
    gpost_ref, out_ref, y_ref, gs_ref, r_ref = refs[4 + has_bias:]
    _set_gains(gs_ref, mod_ref, 1, None, gpost_ref, 1.0)
    y = _dot(o_ref[...], w_ref[...])
    y_ref[...] = y + b_ref[...] if has_bias else y
    _postnorm_rows(y_ref, x_ref, out_ref, gs_ref, r_ref)


def _mixer_out_call(o, xs, mods, norm_post, w, bias, *, layer, li, seq, nb, tm, name):
    n_rows, d = o.shape
    has_bias = bias is not None
    return pl.pallas_call(
        functools.partial(_mixer_out_kernel, has_bias=has_bias),
        grid=(n_rows // tm,),
        in_specs=[
            pl.BlockSpec((tm, d), lambda i: (i, 0)),
            pl.BlockSpec((tm, d), lambda i: (i, 0)),
            _mod_spec(layer, tm, seq, nb, d),
            pl.BlockSpec((None, d, d), lambda i: (li, 0, 0)),
        ] + ([_row_spec(li, d)] if has_bias else []) + [
            _row_spec(layer * N_SUB + 1, d),
        ],
        out_specs=pl.BlockSpec((tm, d), lambda i: (i, 0)),
        out_shape=jax.ShapeDtypeStruct((n_rows, d), _F32),
        scratch_shapes=[pltpu.VMEM((tm, d), _F32), pltpu.VMEM((GS_SLOTS, SUBLANES, d), _F32), pltpu.VMEM((tm, LANES), _F32)],
        compiler_params=_cparams(("parallel",)),
        name=name,
    )(o, xs, mods, w, *([bias] if has_bias else []), norm_post)


def _tiles(n_lat_seq, n_ctx_total, d, f):
    tm = min(512, n_lat_seq, n_ctx_total)
    return dict(tm=tm, fc=min(512, f), nc=min(2048, d), nc_qkv=min(2048, d), tconv=min(256, n_lat_seq),
                tq=min(512, n_lat_seq))


def kernel(x, c, ctx, c_ctx, ada_w, ada_b, norm_pre, norm_post, ffn1_w_in, ffn1_w_out, ffn2_w_in, ffn2_w_out,
           conv_w_pw1, conv_b_pw1, conv_w_dw, conv_b_dw, conv_ln_g, conv_ln_b, conv_w_pw2, conv_b_pw2,
           attn_w_qkv, attn_lambda_q1, attn_lambda_k1, attn_lambda_q2, attn_lambda_k2, attn_subln_g, attn_w_o):
    nb, seq, d = x.shape
    m = ctx.shape[1]
    depth = ada_w.shape[0]
    f = ffn1_w_out.shape[1]
    n_lat, n_ctx = nb * seq, nb * m
    t = _tiles(seq, n_ctx, d, f)
    tm, fc, nc = t["tm"], t["fc"], t["nc"]
    assert seq % tm == 0 and n_ctx % tm == 0 and f % fc == 0 and d % nc == 0 and nc % HEAD_LANES == 0
    assert seq % GRID_W == 0 and d % HEAD_LANES == 0

    bf = lambda w: w.astype(_BF16)
    rows3 = lambda v: v.reshape(-1, 1, v.shape[-1])

    rows = -(-(nb + 1) // SUBLANES) * SUBLANES
    cs = jnp.zeros((rows, d), _F32).at[:nb].set(c).at[nb].set(c_ctx)
    mods = _ada_call(cs, ada_w, ada_b, min(1024, d)).reshape(depth, rows, N_MOD, d)

    g_pre, g_post = rows3(norm_pre), rows3(norm_post)
    ffn_w = ((bf(ffn1_w_in), bf(ffn1_w_out)), (bf(ffn2_w_in), bf(ffn2_w_out)))
    w_pw1, w_pw2, w_qkv, w_o = bf(conv_w_pw1), bf(conv_w_pw2), _rope_layout(bf(attn_w_qkv)), bf(attn_w_o)
    lam_params = jnp.stack([attn_lambda_q1, attn_lambda_k1, attn_lambda_q2, attn_lambda_k2], axis=1)

    xs, tail = x.reshape(n_lat, d), ctx.reshape(n_ctx, d)
    n_all = n_lat + n_ctx

    for i in range(depth):
        last = i == depth - 1
        is_conv = i % 2 == 0
        li = i // 2
        ctx_live = not (last and is_conv)
        common = dict(layer=i, seq=seq, nb=nb)

        xs = _ffn_call(xs, mods, g_pre, g_post, *ffn_w[0], j=0, n_tok=n_all if ctx_live else n_lat,
                       tm=tm, fc=fc, tail=tail if ctx_live else None, **common)
        tail = None

        if is_conv:
            if last and xs.shape[0] != n_lat:
                xs = xs[:n_lat]
            u = _pw1_call(xs, mods, g_pre, w_pw1, rows3(conv_b_pw1), li=li, tm=tm, nc=nc, **common)
            z = _dwconv_call(u, conv_w_dw, rows3(conv_b_dw), rows3(conv_ln_g), rows3(conv_ln_b),
                             li=li, seq=seq, ctx=m, nb=nb, tm=t["tconv"])
            xs = _mixer_out_call(z, xs, mods, g_post, w_pw2, rows3(conv_b_pw2), li=li, tm=tm, name="conv_pw2_out",
                                 **common)
        else:
            assert last, "attention layers before the last one are not supported"
            lam_init = 0.8 - 0.6 * math.exp(-0.3 * i)
            qkv = _qkv_call(xs, mods, g_pre, w_qkv, li=li, tm=tm, nc=t["nc_qkv"], **common)
            o = _attn_call(qkv, lam_params, rows3(attn_subln_g), li=li, lam_init=lam_init, seq=seq, ctx=m, nb=nb,
                           tq=t["tq"])
            xs = _mixer_out_call(o, xs, mods, g_post, w_o, None, li=li, tm=tm, name="attn_out_proj", **common)

        xs = _ffn_call(xs, mods, g_pre, g_post, *ffn_w[1], j=2, n_tok=n_lat if last else n_all,
                       tm=tm, fc=fc, **common)

    return xs[:n_lat].reshape(nb, seq, d)
```
